```python
import math
import jax, jax.numpy as jnp
from jax import lax
import numpy as np

D_MODEL = 2048
BATCH = 1
SEQ = 8192
DEPTH = 4

CHUNK = 64
N_MEM = 256
LN_EPS = 1e-5
RMS_EPS = 1e-6
DEEPNORM_ALPHA = (2.0 * DEPTH) ** 0.25
DEEPNORM_BETA = (8.0 * DEPTH) ** -0.25

D_MIX = D_MODEL
GLA_HEADS = 4
GLA_DV = D_MIX // 2
GLA_HEAD_V = GLA_DV // GLA_HEADS
GLA_DK = GLA_DV // 2
GLA_HEAD_K = GLA_DK // GLA_HEADS
GLA_RANK = 16
GLA_TAU = 16.0
SSD_D_INNER = D_MIX - GLA_DV
SSD_HEADDIM = 64
SSD_HEADS = SSD_D_INNER // SSD_HEADDIM
SSD_STATE = 128
SSD_GROUPS = 2
SSD_CONV = 4
SSD_CONV_DIM = SSD_D_INNER + 2 * SSD_GROUPS * SSD_STATE
DT_MIN = 0.001
DT_MAX = 0.1
MIX_SPLITS = (GLA_DK, GLA_DK, GLA_DV, GLA_DV, GLA_RANK, SSD_D_INNER, SSD_CONV_DIM, SSD_HEADS)
D_IN = GLA_DK * 2 + GLA_DV * 2 + GLA_RANK + SSD_D_INNER + SSD_CONV_DIM + SSD_HEADS
XA_HEADS = 4
XA_HEAD_DIM = D_MODEL // XA_HEADS
D_FF = ((8 * D_MODEL // 3 + 255) // 256) * 256
FFN_CONV = 3

kernel_name = 'hybrid_gla_ssd_memxattn_deepnorm_trunk'


def _layer_norm(x, g, b):
    xf = x.astype(jnp.float32)
    mu = jnp.mean(xf, axis=-1, keepdims=True)
    var = jnp.mean(jnp.square(xf - mu), axis=-1, keepdims=True)
    return ((xf - mu) * lax.rsqrt(var + LN_EPS) * g + b).astype(x.dtype)


def _rms_norm(x, g):
    xf = x.astype(jnp.float32)
    return xf * lax.rsqrt(jnp.mean(jnp.square(xf), axis=-1, keepdims=True) + RMS_EPS) * g


def _causal_dwconv(x, w, b):
    k = w.shape[0]
    y = lax.conv_general_dilated(x, w[:, None, :], window_strides=(1,), padding=[(k - 1, 0)],
                                 dimension_numbers=('NWC', 'WIO', 'NWC'),
                                 feature_group_count=x.shape[-1])
    return y + b


def _chunk_states(decay, contrib):
    dec = jnp.moveaxis(decay, 1, 0)
    con = jnp.moveaxis(contrib, 1, 0)

    def step(state, inp):
        d, c = inp
        return d * state + c, state

    _, states = lax.scan(step, jnp.zeros_like(con[0]), (dec, con))
    return jnp.moveaxis(states, 0, 1)


def _gla(q, k, v, log_g):
    bsz, s, h, dk = q.shape
    dv = v.shape[-1]
    nc = s // CHUNK
    q = q.reshape(bsz, nc, CHUNK, h, dk) * (dk ** -0.5)
    k = k.reshape(bsz, nc, CHUNK, h, dk)
    v = v.reshape(bsz, nc, CHUNK, h, dv)
    b = jnp.cumsum(log_g.reshape(bsz, nc, CHUNK, h, dk), axis=2)
    b_last = b[:, :, -1]
    q_dec = q * jnp.exp(b)
    k_inv = k * jnp.exp(-b)
    k_end = k * jnp.exp(b_last[:, :, None] - b)
    causal = jnp.tril(jnp.ones((CHUNK, CHUNK), dtype=bool))
    scores = jnp.where(causal, jnp.einsum('bclhd,bcmhd->bchlm', q_dec, k_inv), 0.0)
    o_intra = jnp.einsum('bchlm,bcmhv->bclhv', scores, v)
    contrib = jnp.einsum('bclhd,bclhv->bchdv', k_end, v)
    states = _chunk_states(jnp.exp(b_last)[..., None], contrib)
    o_inter = jnp.einsum('bclhd,bchdv->bclhv', q_dec, states)
    return (o_intra + o_inter).reshape(bsz, s, h, dv)


def _ssd(x, dt, a, bm, cm):
    bsz, s, h, p = x.shape
    g, n = bm.shape[-2:]
    hpg = h // g
    nc = s // CHUNK
    xdt = (x * dt[..., None]).reshape(bsz, nc, CHUNK, g, hpg, p)
    cs = jnp.cumsum((dt * a).reshape(bsz, nc, CHUNK, g, hpg), axis=2)
    bc = bm.reshape(bsz, nc, CHUNK, g, n)
    cc = cm.reshape(bsz, nc, CHUNK, g, n)
    causal = jnp.tril(jnp.ones((CHUNK, CHUNK), dtype=bool))
    seg = cs[:, :, :, None] - cs[:, :, None]
    lmat = jnp.exp(jnp.where(causal[:, :, None, None], seg, -jnp.inf))
    cb = jnp.einsum('bclgn,bcmgn->bclmg', cc, bc)
    y_diag = jnp.einsum('bclmg,bclmgh,bcmghp->bclghp', cb, lmat, xdt)
    cs_last = cs[:, :, -1]
    contrib = jnp.einsum('bclgn,bclgh,bclghp->bcghpn', bc, jnp.exp(cs_last[:, :, None] - cs), xdt)
    states = _chunk_states(jnp.exp(cs_last)[..., None, None], contrib)
    y_off = jnp.einsum('bclgn,bcghpn,bclgh->bclghp', cc, states, jnp.exp(cs))
    return (y_diag + y_off).reshape(bsz, s, h, p)


def _hybrid_mixer(x, w_in, gla_w_gate, gla_b_gate, gla_norm_w, ssd_conv_w, ssd_conv_b,
                  ssd_dt_bias, ssd_a_log, ssd_d, ssd_norm_w, w_out):
    f32 = jnp.float32
    bsz, s, _ = x.shape
    split_idx = tuple(int(i) for i in np.cumsum(MIX_SPLITS)[:-1])
    q, k, v, og, a_lr, z, xbc, dt = jnp.split(x @ w_in, split_idx, axis=-1)
    log_g = jax.nn.log_sigmoid((a_lr @ gla_w_gate + gla_b_gate).astype(f32)) / GLA_TAU
    o_gla = _gla(q.astype(f32).reshape(bsz, s, GLA_HEADS, GLA_HEAD_K),
                 k.astype(f32).reshape(bsz, s, GLA_HEADS, GLA_HEAD_K),
                 v.astype(f32).reshape(bsz, s, GLA_HEADS, GLA_HEAD_V),
                 log_g.reshape(bsz, s, GLA_HEADS, GLA_HEAD_K))
    o_gla = _rms_norm(o_gla, gla_norm_w.reshape(GLA_HEADS, GLA_HEAD_V)).reshape(bsz, s, GLA_DV)
    o_gla = o_gla * jax.nn.silu(og.astype(f32))
    xbc = jax.nn.silu(_causal_dwconv(xbc, ssd_conv_w, ssd_conv_b)).astype(f32)
    xs, bm, cm = jnp.split(xbc, [SSD_D_INNER, SSD_D_INNER + SSD_GROUPS * SSD_STATE], axis=-1)
    dt = jax.nn.softplus(dt.astype(f32) + ssd_dt_bias)
    a = -jnp.exp(ssd_a_log.astype(f32))
    xs = xs.reshape(bsz, s, SSD_HEADS, SSD_HEADDIM)
    y = _ssd(xs, dt, a, bm.reshape(bsz, s, SSD_GROUPS, SSD_STATE),
             cm.reshape(bsz, s, SSD_GROUPS, SSD_STATE))
    y = (y + ssd_d[:, None] * xs).reshape(bsz, s, SSD_D_INNER) * jax.nn.silu(z.astype(f32))
    y = _rms_norm(y.reshape(bsz, s, SSD_GROUPS, SSD_D_INNER // SSD_GROUPS),
                  ssd_norm_w.reshape(SSD_GROUPS, SSD_D_INNER // SSD_GROUPS)).reshape(bsz, s, SSD_D_INNER)
    mixed = jnp.concatenate([o_gla, y], axis=-1).astype(x.dtype)
    return mixed @ w_out


def _cross_attention(x, mem, wq, wk, wv, wo):
    bsz, s, d = x.shape
    m = mem.shape[1]
    q = (x @ wq).reshape(bsz, s, XA_HEADS, XA_HEAD_DIM)
    k = (mem @ wk).reshape(bsz, m, XA_HEADS, XA_HEAD_DIM)
    v = (mem @ wv).reshape(bsz, m, XA_HEADS, XA_HEAD_DIM)
    scores = jnp.einsum('bshd,bmhd->bhsm', q, k).astype(jnp.float32) * (XA_HEAD_DIM ** -0.5)
    p = jax.nn.softmax(scores, axis=-1).astype(x.dtype)
    o = jnp.einsum('bhsm,bmhd->bshd', p, v).reshape(bsz, s, d)
    return o @ wo


def _conv_ffn(x, w_up, conv_w, conv_b, w_down):
    gate, val = jnp.split(x @ w_up, 2, axis=-1)
    gate = _causal_dwconv(gate, conv_w, conv_b)
    return (jax.nn.gelu(gate, approximate=False) * val) @ w_down


def setup_inputs(seed: int = 0) -> dict:
    key = jax.random.key(seed)
    ks = jax.random.split(key, 32)
    f32 = jnp.float32
    L = DEPTH

    def nrm(k, shape, scale):
        return jax.random.normal(k, shape, f32) * scale

    segs = ((GLA_DK, 1.0), (GLA_DK, 1.0), (GLA_DV, DEEPNORM_BETA), (GLA_DV, 1.0), (GLA_RANK, 1.0),
            (SSD_D_INNER, 1.0), (SSD_D_INNER, DEEPNORM_BETA), (2 * SSD_GROUPS * SSD_STATE, 1.0),
            (SSD_HEADS, 1.0))
    col_scale = jnp.concatenate([jnp.full((n,), sc, f32) for n, sc in segs])
    dt0 = jnp.exp(jax.random.uniform(ks[8], (L, SSD_HEADS), f32, math.log(DT_MIN), math.log(DT_MAX)))
    return {
        'x': nrm(ks[0], (BATCH, SEQ, D_MODEL), 1.0),
        'mem': nrm(ks[1], (BATCH, N_MEM, D_MODEL), 1.0),
        'w_in': nrm(ks[2], (L, D_MODEL, D_IN), D_MODEL ** -0.5) * col_scale,
        'gla_w_gate': nrm(ks[3], (L, GLA_RANK, GLA_DK), GLA_RANK ** -0.5),
        'gla_b_gate': nrm(ks[4], (L, GLA_DK), 0.02),
        'gla_norm_w': 1.0 + nrm(ks[5], (L, GLA_DV), 0.02),
        'ssd_conv_w': nrm(ks[6], (L, SSD_CONV, SSD_CONV_DIM), SSD_CONV ** -0.5),
        'ssd_conv_b': nrm(ks[7], (L, SSD_CONV_DIM), 0.02),
        'ssd_dt_bias': dt0 + jnp.log(-jnp.expm1(-dt0)),
        'ssd_a_log': jnp.log(jax.random.uniform(ks[9], (L, SSD_HEADS), f32, 1.0, 16.0)),
        'ssd_d': 1.0 + nrm(ks[10], (L, SSD_HEADS), 0.02),
        'ssd_norm_w': 1.0 + nrm(ks[11], (L, SSD_D_INNER), 0.02),
        'w_out': nrm(ks[12], (L, D_MIX, D_MODEL), D_MIX ** -0.5 * DEEPNORM_BETA),
        'ln_mix_g': 1.0 + nrm(ks[13], (L, D_MODEL), 0.02),
        'ln_mix_b': nrm(ks[14], (L, D_MODEL), 0.02),
        'xa_wq': nrm(ks[15], (L, D_MODEL, D_MODEL), D_MODEL ** -0.5),
        'xa_wk': nrm(ks[16], (L, D_MODEL, D_MODEL), D_MODEL ** -0.5),
        'xa_wv': nrm(ks[17], (L, D_MODEL, D_MODEL), D_MODEL ** -0.5 * DEEPNORM_BETA),
        'xa_wo': nrm(ks[18], (L, D_MODEL, D_MODEL), D_MODEL ** -0.5 * DEEPNORM_BETA),
        'ln_xa_g': 1.0 + nrm(ks[19], (L, D_MODEL), 0.02),
        'ln_xa_b': nrm(ks[20], (L, D_MODEL), 0.02),
        'ffn_w_up': nrm(ks[21], (L, D_MODEL, 2 * D_FF), D_MODEL ** -0.5 * DEEPNORM_BETA),
        'ffn_conv_w': nrm(ks[22], (L, FFN_CONV, D_FF), FFN_CONV ** -0.5),
        'ffn_conv_b': nrm(ks[23], (L, D_FF), 0.02),
        'ffn_w_down': nrm(ks[24], (L, D_FF, D_MODEL), D_FF ** -0.5 * DEEPNORM_BETA),
        'ln_ffn_g': 1.0 + nrm(ks[25], (L, D_MODEL), 0.02),
        'ln_ffn_b': nrm(ks[26], (L, D_MODEL), 0.02),
    }


def reference(x, mem, w_in, gla_w_gate, gla_b_gate, gla_norm_w, ssd_conv_w, ssd_conv_b,
              ssd_dt_bias, ssd_a_log, ssd_d, ssd_norm_w, w_out, ln_mix_g, ln_mix_b,
              xa_wq, xa_wk, xa_wv, xa_wo, ln_xa_g, ln_xa_b,
              ffn_w_up, ffn_conv_w, ffn_conv_b, ffn_w_down, ln_ffn_g, ln_ffn_b):
    for l in range(DEPTH):
        h = _hybrid_mixer(x, w_in[l], gla_w_gate[l], gla_b_gate[l], gla_norm_w[l],
                          ssd_conv_w[l], ssd_conv_b[l], ssd_dt_bias[l], ssd_a_log[l],
                          ssd_d[l], ssd_norm_w[l], w_out[l])
        x = _layer_norm(DEEPNORM_ALPHA * x + h, ln_mix_g[l], ln_mix_b[l])
        h = _cross_attention(x, mem, xa_wq[l], xa_wk[l], xa_wv[l], xa_wo[l])
        x = _layer_norm(DEEPNORM_ALPHA * x + h, ln_xa_g[l], ln_xa_b[l])
        h = _conv_ffn(x, ffn_w_up[l], ffn_conv_w[l], ffn_conv_b[l], ffn_w_down[l])
        x = _layer_norm(DEEPNORM_ALPHA * x + h, ln_ffn_g[l], ln_ffn_b[l])
    return x
```

```python
import functools
import math

import jax
import jax.numpy as jnp
import numpy as np
from jax import lax
from jax.experimental import pallas as pl
from jax.experimental.pallas import tpu as pltpu

F32 = jnp.float32
BF16 = jnp.bfloat16

D_MODEL = 2048
SEQ = 8192
DEPTH = 4
CHUNK = 64
N_MEM = 256
LN_EPS = 1e-5
RMS_EPS = 1e-6
ALPHA = (2.0 * DEPTH) ** 0.25

GLA_HEADS = 4
GLA_DV = D_MODEL // 2
GLA_HEAD_V = GLA_DV // GLA_HEADS
GLA_DK = GLA_DV // 2
GLA_HEAD_K = GLA_DK // GLA_HEADS
GLA_RANK = 16
GLA_TAU = 16.0
SSD_D_INNER = D_MODEL - GLA_DV
SSD_HEADDIM = 64
SSD_HEADS = SSD_D_INNER // SSD_HEADDIM
SSD_STATE = 128
SSD_GROUPS = 2
SSD_HPG = SSD_HEADS // SSD_GROUPS
SSD_CONV = 4
SSD_BC = SSD_GROUPS * SSD_STATE
SSD_CONV_DIM = SSD_D_INNER + 2 * SSD_BC
SSD_GROUP_W = SSD_D_INNER // SSD_GROUPS
XA_HEADS = 4
XA_HEAD_DIM = D_MODEL // XA_HEADS
D_FF = ((8 * D_MODEL // 3 + 255) // 256) * 256
FFN_CONV = 3

LANES = 128
SUBLANES = 8
QKVOG_W = 2 * GLA_DK + 2 * GLA_DV
SMALL_W = 2 * LANES
VMEM_LIMIT = 56 * 1024 * 1024

NT = (((1,), (1,)), ((), ()))
TN = (((0,), (0,)), ((), ()))


def _dot(a, b):
    return jnp.dot(a, b, preferred_element_type=F32)


def _dotg(a, b, dims):
    return lax.dot_general(a, b, dims, preferred_element_type=F32)


def _split3(v):
    hi = v.astype(BF16)
    r = v - hi.astype(F32)
    mid = r.astype(BF16)
    lo = (r - mid.astype(F32)).astype(BF16)
    return hi, mid, lo


def _exact_left(mat, v):
    hi, mid, lo = _split3(v)
    return _dot(mat, hi) + _dot(mat, mid) + _dot(mat, lo)


def _exact_right(v, mat):
    hi, mid, lo = _split3(v)
    return _dot(hi, mat) + _dot(mid, mat) + _dot(lo, mat)


def _sigmoid(x):
    return 1.0 / (1.0 + jnp.exp(-x))


def _softplus(x):
    return jnp.maximum(x, 0.0) + jnp.log1p(jnp.exp(-jnp.abs(x)))


def _layer_norm(y, g, b):
    mu = jnp.mean(y, axis=-1, keepdims=True)
    d = y - mu
    var = jnp.mean(d * d, axis=-1, keepdims=True)
    return d * lax.rsqrt(var + LN_EPS) * g + b


def _resident(shape):
    nd = len(shape)
    return pl.BlockSpec(shape, lambda *_: (0,) * nd, pipeline_mode=pl.Buffered(1))


def _params(sem):
    return pltpu.CompilerParams(dimension_semantics=sem, vmem_limit_bytes=VMEM_LIMIT)


def _inproj_kernel(x_ref, w1_ref, wz_ref, wx_ref, ws_ref, o1_ref, oz_ref, ox_ref, os_ref):
    xb = x_ref[...].astype(BF16)
    o1_ref[...] = _dot(xb, w1_ref[...])
    oz_ref[...] = _dot(xb, wz_ref[...])
    ox_ref[...] = _dot(xb, wx_ref[...])
    os_ref[...] = _dot(xb, ws_ref[...])


def _inproj(x, w1, wz, wx, ws, tm=256):
    s, d = x.shape
    widths = (w1.shape[1], wz.shape[1], wx.shape[1], ws.shape[1])
    row = lambda n: pl.BlockSpec((tm, n), lambda i: (i, 0))
    return pl.pallas_call(
        _inproj_kernel,
        grid=(s // tm,),
        in_specs=[row(d)] + [_resident((d, n)) for n in widths],
        out_specs=[row(n) for n in widths],
        out_shape=[jax.ShapeDtypeStruct((s, n), F32) for n in widths],
        compiler_params=_params(("arbitrary",)),
        name="inproj",
    )(x, w1, wz, wx, ws)


def _mixer_kernel(qkvog_ref, z_ref, xbc_ref, sm_ref, wg_ref, bg_ref, gnw_ref, cw_ref, cb_ref,
                  dtb_ref, a_ref, dexp_ref, snw_ref, e_ref, tri_ref, trile_ref, diage_ref, bd_ref,
                  out_ref, sgla_ref, sssd_ref, xprev_ref, xext_ref):
    c = pl.program_id(0)

    @pl.when(c == 0)
    def _():
        sgla_ref[...] = jnp.zeros_like(sgla_ref)
        sssd_ref[...] = jnp.zeros_like(sssd_ref)
        xprev_ref[...] = jnp.zeros_like(xprev_ref)

    tri = tri_ref[...]
    tri_mask = tri > 0

    alr = sm_ref[:, 0:LANES].astype(BF16)
    gpre = _dot(alr, wg_ref[...]) + bg_ref[...]
    log_g = -_softplus(-gpre) / GLA_TAU
    b = _exact_left(tri, log_g)
    b_last = b[CHUNK - 1:CHUNK, :]
    q = qkvog_ref[:, 0:GLA_DK]
    k = qkvog_ref[:, GLA_DK:2 * GLA_DK]
    q_dec = ((q * (GLA_HEAD_K ** -0.5)) * jnp.exp(b)).astype(BF16)
    k_inv = (k * jnp.exp(-b)).astype(BF16)
    k_end = (k * jnp.exp(b_last - b)).astype(BF16)
    e_last = jnp.exp(b_last)
    for h in range(GLA_HEADS):
        ks = slice(h * GLA_HEAD_K, (h + 1) * GLA_HEAD_K)
        vs = slice(2 * GLA_DK + h * GLA_HEAD_V, 2 * GLA_DK + (h + 1) * GLA_HEAD_V)
        gs = slice(2 * GLA_DK + GLA_DV + h * GLA_HEAD_V, 2 * GLA_DK + GLA_DV + (h + 1) * GLA_HEAD_V)
        os_ = slice(h * GLA_HEAD_V, (h + 1) * GLA_HEAD_V)
        vh = qkvog_ref[:, vs].astype(BF16)
        scores = jnp.where(tri_mask, _dotg(q_dec[:, ks], k_inv[:, ks], NT), 0.0)
        state_t = sgla_ref[h]
        o = _dot(scores.astype(BF16), vh) + _dotg(q_dec[:, ks], state_t.astype(BF16), NT)
        sgla_ref[h] = e_last[:, ks] * state_t + _dotg(vh, k_end[:, ks], TN)
        o = o * lax.rsqrt(jnp.mean(o * o, axis=-1, keepdims=True) + RMS_EPS) * gnw_ref[:, os_]
        og = qkvog_ref[:, gs]
        out_ref[:, os_] = (o * (og * _sigmoid(og))).astype(BF16)

    xext_ref[0:SUBLANES, :] = xprev_ref[...]
    xext_ref[SUBLANES:SUBLANES + CHUNK, :] = xbc_ref[...]
    xprev_ref[...] = xbc_ref[CHUNK - SUBLANES:CHUNK, :]
    conv = cb_ref[...]
    for j in range(SSD_CONV):
        off = SUBLANES - (SSD_CONV - 1) + j
        conv = conv + cw_ref[j:j + 1, :] * xext_ref[off:off + CHUNK, :]
    xc = conv * _sigmoid(conv)
    xs = xc[:, 0:SSD_D_INNER]
    dt = _softplus(sm_ref[:, LANES:2 * LANES] + dtb_ref[...])
    cs = _exact_left(tri, dt * -jnp.exp(a_ref[...]))
    expand = e_ref[...]
    dt_e = _exact_right(dt, expand)
    cs_e = _exact_right(cs, expand)
    cs_last_e = cs_e[CHUNK - 1:CHUNK, :]
    xdt = xs * dt_e
    cs_row = jnp.sum(cs_e * diage_ref[...], axis=0, keepdims=True)
    lmat = jnp.where(trile_ref[...] > 0, jnp.exp(cs_e - cs_row), 0.0)
    xdt_end = (xdt * jnp.exp(cs_last_e - cs_e)).astype(BF16)
    xdt_b = xdt.astype(BF16)
    decay_in = jnp.exp(cs_e)
    decay_chunk = jnp.exp(cs_last_e)
    bd = bd_ref[...] > 0
    for g in range(SSD_GROUPS):
        gl = slice(g * SSD_GROUP_W, (g + 1) * SSD_GROUP_W)
        bm = xc[:, SSD_D_INNER + g * SSD_STATE:SSD_D_INNER + (g + 1) * SSD_STATE].astype(BF16)
        cm = xc[:, SSD_D_INNER + SSD_BC + g * SSD_STATE:
                SSD_D_INNER + SSD_BC + (g + 1) * SSD_STATE].astype(BF16)
        bm_rep = jnp.concatenate([bm] * SSD_HPG, axis=0)
        w = (_dotg(cm, bm_rep, NT) * lmat[:, gl]).astype(BF16)
        state_t = sssd_ref[g]
        y_off = _dot(cm, state_t.astype(BF16)) * decay_in[:, gl]
        sssd_ref[g] = decay_chunk[:, gl] * state_t + _dotg(bm, xdt_end[:, gl], TN)
        y_diag = []
        for hp in range(SSD_HPG // 2):
            pl_ = slice(g * SSD_GROUP_W + hp * 2 * SSD_HEADDIM, g * SSD_GROUP_W + (hp + 1) * 2 * SSD_HEADDIM)
            wl = slice(hp * 2 * CHUNK, (hp + 1) * 2 * CHUNK)
            x2 = xdt_b[:, pl_]
            x_bd = jnp.where(bd, jnp.concatenate([x2, x2], axis=0), jnp.zeros((), BF16))
            y_diag.append(_dot(w[:, wl], x_bd))
        y = jnp.concatenate(y_diag, axis=1) + y_off + dexp_ref[:, gl] * xs[:, gl]
        zg = z_ref[:, gl]
        y = y * (zg * _sigmoid(zg))
        y = y * lax.rsqrt(jnp.mean(y * y, axis=-1, keepdims=True) + RMS_EPS) * snw_ref[:, gl]
        out_ref[:, GLA_DV + g * SSD_GROUP_W:GLA_DV + (g + 1) * SSD_GROUP_W] = y.astype(BF16)


def _mixer_constants():
    l = np.arange(CHUNK)
    tri = (l[None, :] <= l[:, None]).astype(np.float32)
    lane = np.arange(SSD_D_INNER)
    m_of_lane = lane % SSD_HEADDIM
    trile = (m_of_lane[None, :] <= l[:, None]).astype(np.float32)
    diage = (m_of_lane[None, :] == l[:, None]).astype(np.float32)
    expand = np.zeros((LANES, SSD_D_INNER), np.float32)
    expand[lane // SSD_HEADDIM, lane] = 1.0
    r = np.arange(2 * SSD_HEADDIM)
    bd = ((r[:, None] // SSD_HEADDIM) == (r[None, :] // SSD_HEADDIM)).astype(np.float32)
    return (jnp.asarray(expand, BF16), jnp.asarray(tri, BF16), jnp.asarray(trile), jnp.asarray(diage),
            jnp.asarray(bd))


def _mixer(qkvog, z, xbc, sm, wg, bg, gnw, cw, cb, dtb, a_row, dexp, snw):
    assert CHUNK == SSD_HEADDIM
    s = qkvog.shape[0]
    consts = _mixer_constants()
    row = lambda n: pl.BlockSpec((CHUNK, n), lambda i: (i, 0))
    full = lambda arr: pl.BlockSpec(arr.shape, lambda i: (0,) * arr.ndim)
    small = (wg, bg, gnw, cw, cb, dtb, a_row, dexp, snw) + consts
    return pl.pallas_call(
        _mixer_kernel,
        grid=(s // CHUNK,),
        in_specs=[row(QKVOG_W), row(SSD_D_INNER), row(SSD_CONV_DIM), row(SMALL_W)] + [full(a) for a in small],
        out_specs=row(D_MODEL),
        out_shape=jax.ShapeDtypeStruct((s, D_MODEL), BF16),
        scratch_shapes=[
            pltpu.VMEM((GLA_HEADS, GLA_HEAD_V, GLA_HEAD_K), F32),
            pltpu.VMEM((SSD_GROUPS, SSD_STATE, SSD_GROUP_W), F32),
            pltpu.VMEM((SUBLANES, SSD_CONV_DIM), F32),
            pltpu.VMEM((SUBLANES + CHUNK, SSD_CONV_DIM), F32),
        ],
        compiler_params=_params(("arbitrary",)),
        name="mixer",
    )(qkvog, z, xbc, sm, *small)


def _proj_ln_kernel(a_ref, w_ref, x_ref, g_ref, b_ref, o_ref):
    h = _dot(a_ref[...], w_ref[...])
    o_ref[...] = _layer_norm(ALPHA * x_ref[...] + h, g_ref[...], b_ref[...])


def _proj_ln(a, w, x, g, b, tm=512):
    s, d = x.shape
    kdim = a.shape[1]
    return pl.pallas_call(
        _proj_ln_kernel,
        grid=(s // tm,),
        in_specs=[pl.BlockSpec((tm, kdim), lambda i: (i, 0)), _resident((kdim, d)),
                  pl.BlockSpec((tm, d), lambda i: (i, 0)), _resident((1, d)), _resident((1, d))],
        out_specs=pl.BlockSpec((tm, d), lambda i: (i, 0)),
        out_shape=jax.ShapeDtypeStruct((s, d), F32),
        compiler_params=_params(("arbitrary",)),
        name="proj_ln",
    )(a, w, x, g, b)


def _kv_kernel(m_ref, w_ref, o_ref):
    o_ref[...] = _dot(m_ref[...].astype(BF16), w_ref[...]).astype(BF16)


def _kv_proj(mem, wkv, tn=1024):
    m, d = mem.shape
    n = wkv.shape[1]
    return pl.pallas_call(
        _kv_kernel,
        grid=(n // tn,),
        in_specs=[pl.BlockSpec((m, d), lambda j: (0, 0)), pl.BlockSpec((d, tn), lambda j: (0, j))],
        out_specs=pl.BlockSpec((m, tn), lambda j: (0, j)),
        out_shape=jax.ShapeDtypeStruct((m, n), BF16),
        compiler_params=_params(("arbitrary",)),
        name="kv_proj",
    )(mem, wkv)


def _xattn_kernel(x_ref, wq_ref, kv_ref, wo_ref, g_ref, b_ref, o_ref, att_ref):
    x = x_ref[...]
    q = _dot(x.astype(BF16), wq_ref[...])
    for h in range(XA_HEADS):
        hs = slice(h * XA_HEAD_DIM, (h + 1) * XA_HEAD_DIM)
        kh = kv_ref[:, hs]
        vh = kv_ref[:, D_MODEL + h * XA_HEAD_DIM:D_MODEL + (h + 1) * XA_HEAD_DIM]
        sc = _dotg(q[:, hs].astype(BF16), kh, NT) * (XA_HEAD_DIM ** -0.5)
        e = jnp.exp(sc - jnp.max(sc, axis=-1, keepdims=True))
        p = e / jnp.sum(e, axis=-1, keepdims=True)
        att_ref[:, hs] = _dot(p.astype(BF16), vh).astype(BF16)
    h_out = _dot(att_ref[...], wo_ref[...])
    o_ref[...] = _layer_norm(ALPHA * x + h_out, g_ref[...], b_ref[...])


def _xattn(x, wq, kv, wo, g, b, tm=512):
    s, d = x.shape
    return pl.pallas_call(
        _xattn_kernel,
        grid=(s // tm,),
        in_specs=[pl.BlockSpec((tm, d), lambda i: (i, 0)), _resident((d, d)), _resident(kv.shape),
                  _resident((d, d)), _resident((1, d)), _resident((1, d))],
        out_specs=pl.BlockSpec((tm, d), lambda i: (i, 0)),
        out_shape=jax.ShapeDtypeStruct((s, d), F32),
        scratch_shapes=[pltpu.VMEM((tm, d), BF16)],
        compiler_params=_params(("arbitrary",)),
        name="xattn",
    )(x, wq, kv, wo, g, b)


def _ffn_kernel(x_ref, wg_ref, wv_ref, wd_ref, cw_ref, cb_ref, g_ref, b_ref, o_ref,
                xb_ref, acc_ref, carry_ref, gext_ref):
    i = pl.program_id(0)
    j = pl.program_id(1)
    tm = x_ref.shape[0]

    @pl.when(j == 0)
    def _():
        xb_ref[...] = x_ref[...].astype(BF16)
        acc_ref[...] = jnp.zeros_like(acc_ref)

    @pl.when(i == 0)
    def _():
        carry_ref[j] = jnp.zeros(carry_ref.shape[1:], F32)

    xb = xb_ref[...]
    gate = _dot(xb, wg_ref[...])
    val = _dot(xb, wv_ref[...])
    gext_ref[0:SUBLANES, :] = carry_ref[j]
    gext_ref[SUBLANES:SUBLANES + tm, :] = gate
    carry_ref[j] = gate[tm - SUBLANES:tm, :]
    conv = cb_ref[...] + cw_ref[FFN_CONV - 1:FFN_CONV, :] * gate
    for t in range(FFN_CONV - 1):
        off = SUBLANES - (FFN_CONV - 1) + t
        conv = conv + cw_ref[t:t + 1, :] * gext_ref[off:off + tm, :]
    act = 0.5 * conv * (1.0 + lax.erf(conv * math.sqrt(0.5)))
    acc_ref[...] += _dot((act * val).astype(BF16), wd_ref[...])

    @pl.when(j == pl.num_programs(1) - 1)
    def _():
        o_ref[...] = _layer_norm(ALPHA * x_ref[...] + acc_ref[...], g_ref[...], b_ref[...])


def _ffn(x, wg, wv, wd, cw, cb, g, b, tm=512, tf=512):
    s, d = x.shape
    f = wg.shape[1]
    return pl.pallas_call(
        _ffn_kernel,
        grid=(s // tm, f // tf),
        in_specs=[pl.BlockSpec((tm, d), lambda i, j: (i, 0)),
                  pl.BlockSpec((d, tf), lambda i, j: (0, j)),
                  pl.BlockSpec((d, tf), lambda i, j: (0, j)),
                  pl.BlockSpec((tf, d), lambda i, j: (j, 0)),
                  pl.BlockSpec((FFN_CONV, tf), lambda i, j: (0, j)),
                  pl.BlockSpec((1, tf), lambda i, j: (0, j)),
                  pl.BlockSpec((1, d), lambda i, j: (0, 0)),
                  pl.BlockSpec((1, d), lambda i, j: (0, 0))],
        out_specs=pl.BlockSpec((tm, d), lambda i, j: (i, 0)),
        out_shape=jax.ShapeDtypeStruct((s, d), F32),
        scratch_shapes=[pltpu.VMEM((tm, d), BF16), pltpu.VMEM((tm, d), F32),
                        pltpu.VMEM((f // tf, SUBLANES, tf), F32), pltpu.VMEM((SUBLANES + tm, tf), F32)],
        compiler_params=_params(("arbitrary", "arbitrary")),
        name="ffn",
    )(x, wg, wv, wd, cw, cb, g, b)


def _pad_cols(w, n):
    return jnp.pad(w, ((0, 0), (0, n - w.shape[1])))


def kernel(x, mem, w_in, gla_w_gate, gla_b_gate, gla_norm_w, ssd_conv_w, ssd_conv_b, ssd_dt_bias, ssd_a_log, ssd_d, ssd_norm_w, w_out, ln_mix_g, ln_mix_b, xa_wq, xa_wk, xa_wv, xa_wo, ln_xa_g, ln_xa_b, ffn_w_up, ffn_conv_w, ffn_conv_b, ffn_w_down, ln_ffn_g, ln_ffn_b):
    assert x.shape == (1, SEQ, D_MODEL) and mem.shape == (1, N_MEM, D_MODEL)
    xs = x[0]
    mem2 = mem[0]
    c_lr = QKVOG_W
    c_z = c_lr + GLA_RANK
    c_xbc = c_z + SSD_D_INNER
    c_dt = c_xbc + SSD_CONV_DIM
    row = lambda v: v.reshape(1, -1)
    for l in range(DEPTH):
        wl = w_in[l]
        w1 = wl[:, :c_lr].astype(BF16)
        wz = wl[:, c_z:c_xbc].astype(BF16)
        wx = wl[:, c_xbc:c_dt].astype(BF16)
        ws = jnp.concatenate([_pad_cols(wl[:, c_lr:c_z], LANES), _pad_cols(wl[:, c_dt:], LANES)], axis=1).astype(BF16)
        wg = jnp.pad(gla_w_gate[l], ((0, LANES - GLA_RANK), (0, 0))).astype(BF16)
        dtb = _pad_cols(row(ssd_dt_bias[l]), LANES)
        a_row = _pad_cols(row(ssd_a_log[l]), LANES)
        dexp = row(jnp.repeat(ssd_d[l], SSD_HEADDIM))

        qkvog, z, xbc, sm = _inproj(xs, w1, wz, wx, ws)
        mixed = _mixer(qkvog, z, xbc, sm, wg, row(gla_b_gate[l]), row(gla_norm_w[l]), ssd_conv_w[l],
                       row(ssd_conv_b[l]), dtb, a_row, dexp, row(ssd_norm_w[l]))
        xs = _proj_ln(mixed, w_out[l].astype(BF16), xs, row(ln_mix_g[l]), row(ln_mix_b[l]))

        wkv = jnp.concatenate([xa_wk[l], xa_wv[l]], axis=1).astype(BF16)
        kv = _kv_proj(mem2, wkv)
        xs = _xattn(xs, xa_wq[l].astype(BF16), kv, xa_wo[l].astype(BF16), row(ln_xa_g[l]), row(ln_xa_b[l]))

        wup = ffn_w_up[l]
        xs = _ffn(xs, wup[:, :D_FF].astype(BF16), wup[:, D_FF:].astype(BF16), ffn_w_down[l].astype(BF16),
                  ffn_conv_w[l], row(ffn_conv_b[l]), row(ln_ffn_g[l]), row(ln_ffn_b[l]))
    return xs[None]
```

```python
import functools
import math

import jax
import jax.numpy as jnp
import numpy as np
from jax import lax
from jax.experimental import pallas as pl
from jax.experimental.pallas import tpu as pltpu

F32 = jnp.float32
BF16 = jnp.bfloat16

D_MODEL = 2048
SEQ = 8192
DEPTH = 4
CHUNK = 64
N_MEM = 256
LN_EPS = 1e-5
RMS_EPS = 1e-6
ALPHA = (2.0 * DEPTH) ** 0.25

GLA_HEADS = 4
GLA_DV = D_MODEL // 2
GLA_HEAD_V = GLA_DV // GLA_HEADS
GLA_DK = GLA_DV // 2
GLA_HEAD_K = GLA_DK // GLA_HEADS
GLA_RANK = 16
GLA_TAU = 16.0
SSD_D_INNER = D_MODEL - GLA_DV
SSD_HEADDIM = 64
SSD_HEADS = SSD_D_INNER // SSD_HEADDIM
SSD_STATE = 128
SSD_GROUPS = 2
SSD_HPG = SSD_HEADS // SSD_GROUPS
SSD_CONV = 4
SSD_BC = SSD_GROUPS * SSD_STATE
SSD_CONV_DIM = SSD_D_INNER + 2 * SSD_BC
SSD_GROUP_W = SSD_D_INNER // SSD_GROUPS
XA_HEADS = 4
XA_HEAD_DIM = D_MODEL // XA_HEADS
D_FF = ((8 * D_MODEL // 3 + 255) // 256) * 256
FFN_CONV = 3

LANES = 128
SUBLANES = 8
QKVOG_W = 2 * GLA_DK + 2 * GLA_DV
SMALL_W = 2 * LANES
D_IN = QKVOG_W + GLA_RANK + SSD_D_INNER + SSD_CONV_DIM + SSD_HEADS
W_IN_PAD = -(-D_IN // LANES) * LANES
Z_END_PAD = -(-(QKVOG_W + GLA_RANK + SSD_D_INNER) // LANES) * LANES
XBC_START_ALIGNED = (QKVOG_W + GLA_RANK + SSD_D_INNER) // LANES * LANES
DT_LANE = D_IN - SSD_HEADS - (W_IN_PAD - LANES)
VMEM_LIMIT = 56 * 1024 * 1024

NT = (((1,), (1,)), ((), ()))
TN = (((0,), (0,)), ((), ()))


def _dot(a, b):
    return jnp.dot(a, b, preferred_element_type=F32)


def _dotg(a, b, dims):
    return lax.dot_general(a, b, dims, preferred_element_type=F32)


def _split3(v):
    hi = v.astype(BF16)
    r = v - hi.astype(F32)
    mid = r.astype(BF16)
    lo = (r - mid.astype(F32)).astype(BF16)
    return hi, mid, lo


def _exact_left(mat, v):
    hi, mid, lo = _split3(v)
    return _dot(mat, hi) + _dot(mat, mid) + _dot(mat, lo)


def _exact_right(v, mat):
    hi, mid, lo = _split3(v)
    return _dot(hi, mat) + _dot(mid, mat) + _dot(lo, mat)


def _sigmoid(x):
    return 1.0 / (1.0 + jnp.exp(-x))


def _softplus(x):
    return jnp.maximum(x, 0.0) + jnp.log1p(jnp.exp(-jnp.abs(x)))


def _layer_norm(y, g, b):
    mu = jnp.mean(y, axis=-1, keepdims=True)
    d = y - mu
    var = jnp.mean(d * d, axis=-1, keepdims=True)
    return d * lax.rsqrt(var + LN_EPS) * g + b


def _resident(shape):
    nd = len(shape)
    return pl.BlockSpec(shape, lambda *_: (0,) * nd, pipeline_mode=pl.Buffered(1))


def _params(sem):
    return pltpu.CompilerParams(dimension_semantics=sem, vmem_limit_bytes=VMEM_LIMIT)


def _inproj_kernel(x_ref, w_ref, o1_ref, oz_ref, ox_ref, os_ref):
    xb = x_ref[...].astype(BF16)
    o1_ref[...] = _dot(xb, w_ref[:, 0:QKVOG_W])
    lo = _dot(xb, w_ref[:, QKVOG_W:Z_END_PAD])
    os_ref[:, 0:LANES] = lo[:, 0:LANES]
    oz_ref[...] = lo[:, GLA_RANK:GLA_RANK + SSD_D_INNER]
    hi = _dot(xb, w_ref[:, XBC_START_ALIGNED:W_IN_PAD])
    ox_ref[...] = hi[:, GLA_RANK:GLA_RANK + SSD_CONV_DIM]
    os_ref[:, LANES:2 * LANES] = hi[:, W_IN_PAD - XBC_START_ALIGNED - LANES:W_IN_PAD - XBC_START_ALIGNED]


def _inproj(x, w, tm=256):
    s, d = x.shape
    widths = (QKVOG_W, SSD_D_INNER, SSD_CONV_DIM, SMALL_W)
    row = lambda n: pl.BlockSpec((tm, n), lambda i: (i, 0))
    return pl.pallas_call(
        _inproj_kernel,
        grid=(s // tm,),
        in_specs=[row(d), _resident(w.shape)],
        out_specs=[row(n) for n in widths],
        out_shape=[jax.ShapeDtypeStruct((s, n), F32) for n in widths],
        compiler_params=_params(("arbitrary",)),
        name="inproj",
    )(x, w)


def _mixer_kernel(qkvog_ref, z_ref, xbc_ref, sm_ref, wg_ref, bg_ref, gnw_ref, cw_ref, cb_ref,
                  dtb_ref, a_ref, dexp_ref, snw_ref, e_ref, tri_ref, trile_ref, diage_ref, bd_ref,
                  out_ref, sgla_ref, sssd_ref, xprev_ref, xext_ref):
    c = pl.program_id(0)

    @pl.when(c == 0)
    def _():
        sgla_ref[...] = jnp.zeros_like(sgla_ref)
        sssd_ref[...] = jnp.zeros_like(sssd_ref)
        xprev_ref[...] = jnp.zeros_like(xprev_ref)

    tri = tri_ref[...]
    tri_mask = tri > 0

    alr = sm_ref[:, 0:LANES].astype(BF16)
    gpre = _dot(alr, wg_ref[...]) + bg_ref[...]
    log_g = -_softplus(-gpre) / GLA_TAU
    b = _exact_left(tri, log_g)
    b_last = b[CHUNK - 1:CHUNK, :]
    q = qkvog_ref[:, 0:GLA_DK]
    k = qkvog_ref[:, GLA_DK:2 * GLA_DK]
    q_dec = ((q * (GLA_HEAD_K ** -0.5)) * jnp.exp(b)).astype(BF16)
    k_inv = (k * jnp.exp(-b)).astype(BF16)
    k_end = (k * jnp.exp(b_last - b)).astype(BF16)
    e_last = jnp.exp(b_last)
    for h in range(GLA_HEADS):
        ks = slice(h * GLA_HEAD_K, (h + 1) * GLA_HEAD_K)
        vs = slice(2 * GLA_DK + h * GLA_HEAD_V, 2 * GLA_DK + (h + 1) * GLA_HEAD_V)
        gs = slice(2 * GLA_DK + GLA_DV + h * GLA_HEAD_V, 2 * GLA_DK + GLA_DV + (h + 1) * GLA_HEAD_V)
        os_ = slice(h * GLA_HEAD_V, (h + 1) * GLA_HEAD_V)
        vh = qkvog_ref[:, vs].astype(BF16)
        scores = jnp.where(tri_mask, _dotg(q_dec[:, ks], k_inv[:, ks], NT), 0.0)
        state_t = sgla_ref[h]
        o = _dot(scores.astype(BF16), vh) + _dotg(q_dec[:, ks], state_t.astype(BF16), NT)
        sgla_ref[h] = e_last[:, ks] * state_t + _dotg(vh, k_end[:, ks], TN)
        o = o * lax.rsqrt(jnp.mean(o * o, axis=-1, keepdims=True) + RMS_EPS) * gnw_ref[:, os_]
        og = qkvog_ref[:, gs]
        out_ref[:, os_] = (o * (og * _sigmoid(og))).astype(BF16)

    xext_ref[0:SUBLANES, :] = xprev_ref[...]
    xext_ref[SUBLANES:SUBLANES + CHUNK, :] = xbc_ref[...]
    xprev_ref[...] = xbc_ref[CHUNK - SUBLANES:CHUNK, :]
    conv = cb_ref[...]
    for j in range(SSD_CONV):
        off = SUBLANES - (SSD_CONV - 1) + j
        conv = conv + cw_ref[j:j + 1, :] * xext_ref[off:off + CHUNK, :]
    xc = conv * _sigmoid(conv)
    xs = xc[:, 0:SSD_D_INNER]
    dt = _softplus(sm_ref[:, LANES:2 * LANES] + dtb_ref[...])
    cs = _exact_left(tri, dt * -jnp.exp(a_ref[...]))
    expand = e_ref[...]
    dt_e = _exact_right(dt, expand)
    cs_e = _exact_right(cs, expand)
    cs_last_e = cs_e[CHUNK - 1:CHUNK, :]
    xdt = xs * dt_e
    cs_row = jnp.sum(cs_e * diage_ref[...], axis=0, keepdims=True)
    lmat = jnp.where(trile_ref[...] > 0, jnp.exp(cs_e - cs_row), 0.0)
    xdt_end = (xdt * jnp.exp(cs_last_e - cs_e)).astype(BF16)
    xdt_b = xdt.astype(BF16)
    decay_in = jnp.exp(cs_e)
    decay_chunk = jnp.exp(cs_last_e)
    bd = bd_ref[...] > 0
    for g in range(SSD_GROUPS):
        gl = slice(g * SSD_GROUP_W, (g + 1) * SSD_GROUP_W)
        bm = xc[:, SSD_D_INNER + g * SSD_STATE:SSD_D_INNER + (g + 1) * SSD_STATE].astype(BF16)
        cm = xc[:, SSD_D_INNER + SSD_BC + g * SSD_STATE:
                SSD_D_INNER + SSD_BC + (g + 1) * SSD_STATE].astype(BF16)
        bm_rep = jnp.concatenate([bm] * SSD_HPG, axis=0)
        w = (_dotg(cm, bm_rep, NT) * lmat[:, gl]).astype(BF16)
        state_t = sssd_ref[g]
        y_off = _dot(cm, state_t.astype(BF16)) * decay_in[:, gl]
        sssd_ref[g] = decay_chunk[:, gl] * state_t + _dotg(bm, xdt_end[:, gl], TN)
        y_diag = []
        for hp in range(SSD_HPG // 2):
            pl_ = slice(g * SSD_GROUP_W + hp * 2 * SSD_HEADDIM, g * SSD_GROUP_W + (hp + 1) * 2 * SSD_HEADDIM)
            wl = slice(hp * 2 * CHUNK, (hp + 1) * 2 * CHUNK)
            x2 = xdt_b[:, pl_]
            x_bd = jnp.where(bd, jnp.concatenate([x2, x2], axis=0), jnp.zeros((), BF16))
            y_diag.append(_dot(w[:, wl], x_bd))
        y = jnp.concatenate(y_diag, axis=1) + y_off + dexp_ref[:, gl] * xs[:, gl]
        zg = z_ref[:, gl]
        y = y * (zg * _sigmoid(zg))
        y = y * lax.rsqrt(jnp.mean(y * y, axis=-1, keepdims=True) + RMS_EPS) * snw_ref[:, gl]
        out_ref[:, GLA_DV + g * SSD_GROUP_W:GLA_DV + (g + 1) * SSD_GROUP_W] = y.astype(BF16)


def _mixer_constants():
    l = np.arange(CHUNK)
    tri = (l[None, :] <= l[:, None]).astype(np.float32)
    lane = np.arange(SSD_D_INNER)
    m_of_lane = lane % SSD_HEADDIM
    trile = (m_of_lane[None, :] <= l[:, None]).astype(np.float32)
    diage = (m_of_lane[None, :] == l[:, None]).astype(np.float32)
    expand = np.zeros((LANES, SSD_D_INNER), np.float32)
    expand[DT_LANE + lane // SSD_HEADDIM, lane] = 1.0
    r = np.arange(2 * SSD_HEADDIM)
    bd = ((r[:, None] // SSD_HEADDIM) == (r[None, :] // SSD_HEADDIM)).astype(np.float32)
    return (jnp.asarray(expand, BF16), jnp.asarray(tri, BF16), jnp.asarray(trile), jnp.asarray(diage),
            jnp.asarray(bd))


def _mixer(qkvog, z, xbc, sm, wg, bg, gnw, cw, cb, dtb, a_row, dexp, snw):
    assert CHUNK == SSD_HEADDIM
    s = qkvog.shape[0]
    consts = _mixer_constants()
    row = lambda n: pl.BlockSpec((CHUNK, n), lambda i: (i, 0))
    full = lambda arr: pl.BlockSpec(arr.shape, lambda i: (0,) * arr.ndim)
    small = (wg, bg, gnw, cw, cb, dtb, a_row, dexp, snw) + consts
    return pl.pallas_call(
        _mixer_kernel,
        grid=(s // CHUNK,),
        in_specs=[row(QKVOG_W), row(SSD_D_INNER), row(SSD_CONV_DIM), row(SMALL_W)] + [full(a) for a in small],
        out_specs=row(D_MODEL),
        out_shape=jax.ShapeDtypeStruct((s, D_MODEL), BF16),
        scratch_shapes=[
            pltpu.VMEM((GLA_HEADS, GLA_HEAD_V, GLA_HEAD_K), F32),
            pltpu.VMEM((SSD_GROUPS, SSD_STATE, SSD_GROUP_W), F32),
            pltpu.VMEM((SUBLANES, SSD_CONV_DIM), F32),
            pltpu.VMEM((SUBLANES + CHUNK, SSD_CONV_DIM), F32),
        ],
        compiler_params=_params(("arbitrary",)),
        name="mixer",
    )(qkvog, z, xbc, sm, *small)


def _proj_ln_kernel(a_ref, w_ref, x_ref, g_ref, b_ref, o_ref):
    h = _dot(a_ref[...], w_ref[...])
    o_ref[...] = _layer_norm(ALPHA * x_ref[...] + h, g_ref[...], b_ref[...])


def _proj_ln(a, w, x, g, b, tm=512):
    s, d = x.shape
    kdim = a.shape[1]
    return pl.pallas_call(
        _proj_ln_kernel,
        grid=(s // tm,),
        in_specs=[pl.BlockSpec((tm, kdim), lambda i: (i, 0)), _resident((kdim, d)),
                  pl.BlockSpec((tm, d), lambda i: (i, 0)), _resident((1, d)), _resident((1, d))],
        out_specs=pl.BlockSpec((tm, d), lambda i: (i, 0)),
        out_shape=jax.ShapeDtypeStruct((s, d), F32),
        compiler_params=_params(("arbitrary",)),
        name="proj_ln",
    )(a, w, x, g, b)


def _kv_kernel(m_ref, wk_ref, wv_ref, k_ref, v_ref):
    mb = m_ref[...].astype(BF16)
    k_ref[...] = _dot(mb, wk_ref[...].astype(BF16)).astype(BF16)
    v_ref[...] = _dot(mb, wv_ref[...].astype(BF16)).astype(BF16)


def _kv_proj(mem, wk, wv, tn=512):
    m, d = mem.shape
    n = wk.shape[1]
    col = lambda rows: pl.BlockSpec((rows, tn), lambda j: (0, j))
    return pl.pallas_call(
        _kv_kernel,
        grid=(n // tn,),
        in_specs=[pl.BlockSpec((m, d), lambda j: (0, 0)), col(d), col(d)],
        out_specs=[col(m), col(m)],
        out_shape=[jax.ShapeDtypeStruct((m, n), BF16)] * 2,
        compiler_params=_params(("arbitrary",)),
        name="kv_proj",
    )(mem, wk, wv)


def _xattn_kernel(x_ref, wq_ref, k_ref, v_ref, wo_ref, g_ref, b_ref, o_ref, att_ref):
    x = x_ref[...]
    q = _dot(x.astype(BF16), wq_ref[...])
    for h in range(XA_HEADS):
        hs = slice(h * XA_HEAD_DIM, (h + 1) * XA_HEAD_DIM)
        sc = _dotg(q[:, hs].astype(BF16), k_ref[:, hs], NT) * (XA_HEAD_DIM ** -0.5)
        e = jnp.exp(sc - jnp.max(sc, axis=-1, keepdims=True))
        p = e / jnp.sum(e, axis=-1, keepdims=True)
        att_ref[:, hs] = _dot(p.astype(BF16), v_ref[:, hs]).astype(BF16)
    h_out = _dot(att_ref[...], wo_ref[...])
    o_ref[...] = _layer_norm(ALPHA * x + h_out, g_ref[...], b_ref[...])


def _xattn(x, wq, k, v, wo, g, b, tm=512):
    s, d = x.shape
    return pl.pallas_call(
        _xattn_kernel,
        grid=(s // tm,),
        in_specs=[pl.BlockSpec((tm, d), lambda i: (i, 0)), _resident((d, d)), _resident(k.shape),
                  _resident(v.shape), _resident((d, d)), _resident((1, d)), _resident((1, d))],
        out_specs=pl.BlockSpec((tm, d), lambda i: (i, 0)),
        out_shape=jax.ShapeDtypeStruct((s, d), F32),
        scratch_shapes=[pltpu.VMEM((tm, d), BF16)],
        compiler_params=_params(("arbitrary",)),
        name="xattn",
    )(x, wq, k, v, wo, g, b)


def _ffn_kernel(x_ref, wg_ref, wv_ref, wd_ref, cw_ref, cb_ref, g_ref, b_ref, o_ref,
                xb_ref, acc_ref, carry_ref, gext_ref):
    i = pl.program_id(0)
    j = pl.program_id(1)
    tm = x_ref.shape[0]

    @pl.when(j == 0)
    def _():
        xb_ref[...] = x_ref[...].astype(BF16)
        acc_ref[...] = jnp.zeros_like(acc_ref)

    @pl.when(i == 0)
    def _():
        carry_ref[j] = jnp.zeros(carry_ref.shape[1:], F32)

    xb = xb_ref[...]
    gate = _dot(xb, wg_ref[...])
    val = _dot(xb, wv_ref[...])
    gext_ref[0:SUBLANES, :] = carry_ref[j]
    gext_ref[SUBLANES:SUBLANES + tm, :] = gate
    carry_ref[j] = gate[tm - SUBLANES:tm, :]
    conv = cb_ref[...] + cw_ref[FFN_CONV - 1:FFN_CONV, :] * gate
    for t in range(FFN_CONV - 1):
        off = SUBLANES - (FFN_CONV - 1) + t
        conv = conv + cw_ref[t:t + 1, :] * gext_ref[off:off + tm, :]
    act = 0.5 * conv * (1.0 + lax.erf(conv * math.sqrt(0.5)))
    acc_ref[...] += _dot((act * val).astype(BF16), wd_ref[...])

    @pl.when(j == pl.num_programs(1) - 1)
    def _():
        o_ref[...] = _layer_norm(ALPHA * x_ref[...] + acc_ref[...], g_ref[...], b_ref[...])


def _ffn(x, wup, wd, cw, cb, g, b, tm=512, tf=512):
    s, d = x.shape
    f = wd.shape[0]
    nf = f // tf
    return pl.pallas_call(
        _ffn_kernel,
        grid=(s // tm, nf),
        in_specs=[pl.BlockSpec((tm, d), lambda i, j: (i, 0)),
                  pl.BlockSpec((d, tf), lambda i, j: (0, j)),
                  pl.BlockSpec((d, tf), lambda i, j: (0, j + nf)),
                  pl.BlockSpec((tf, d), lambda i, j: (j, 0)),
                  pl.BlockSpec((FFN_CONV, tf), lambda i, j: (0, j)),
                  pl.BlockSpec((1, tf), lambda i, j: (0, j)),
                  pl.BlockSpec((1, d), lambda i, j: (0, 0)),
                  pl.BlockSpec((1, d), lambda i, j: (0, 0))],
        out_specs=pl.BlockSpec((tm, d), lambda i, j: (i, 0)),
        out_shape=jax.ShapeDtypeStruct((s, d), F32),
        scratch_shapes=[pltpu.VMEM((tm, d), BF16), pltpu.VMEM((tm, d), F32),
                        pltpu.VMEM((nf, SUBLANES, tf), F32), pltpu.VMEM((SUBLANES + tm, tf), F32)],
        compiler_params=_params(("arbitrary", "arbitrary")),
        name="ffn",
    )(x, wup, wup, wd, cw, cb, g, b)


def _lane_row(v, start):
    return jnp.pad(v.reshape(1, -1), ((0, 0), (start, LANES - start - v.shape[0])))


def kernel(x, mem, w_in, gla_w_gate, gla_b_gate, gla_norm_w, ssd_conv_w, ssd_conv_b, ssd_dt_bias, ssd_a_log, ssd_d, ssd_norm_w, w_out, ln_mix_g, ln_mix_b, xa_wq, xa_wk, xa_wv, xa_wo, ln_xa_g, ln_xa_b, ffn_w_up, ffn_conv_w, ffn_conv_b, ffn_w_down, ln_ffn_g, ln_ffn_b):
    assert x.shape == (1, SEQ, D_MODEL) and mem.shape == (1, N_MEM, D_MODEL)
    xs = x[0]
    mem2 = mem[0]
    assert w_in.shape[-1] == D_IN
    row = lambda v: v.reshape(1, -1)
    for l in range(DEPTH):
        wb = jnp.pad(w_in[l], ((0, 0), (0, W_IN_PAD - D_IN))).astype(BF16)
        wg = jnp.pad(gla_w_gate[l], ((0, LANES - GLA_RANK), (0, 0))).astype(BF16)
        dtb = _lane_row(ssd_dt_bias[l], DT_LANE)
        a_row = _lane_row(ssd_a_log[l], DT_LANE)
        dexp = row(jnp.repeat(ssd_d[l], SSD_HEADDIM))

        qkvog, z, xbc, sm = _inproj(xs, wb)
        mixed = _mixer(qkvog, z, xbc, sm, wg, row(gla_b_gate[l]), row(gla_norm_w[l]), ssd_conv_w[l],
                       row(ssd_conv_b[l]), dtb, a_row, dexp, row(ssd_norm_w[l]))
        xs = _proj_ln(mixed, w_out[l].astype(BF16), xs, row(ln_mix_g[l]), row(ln_mix_b[l]))

        k, v = _kv_proj(mem2, xa_wk[l], xa_wv[l])
        xs = _xattn(xs, xa_wq[l].astype(BF16), k, v, xa_wo[l].astype(BF16), row(ln_xa_g[l]), row(ln_xa_b[l]))

        xs = _ffn(xs, ffn_w_up[l].astype(BF16), ffn_w_down[l].astype(BF16),
                  ffn_conv_w[l], row(ffn_conv_b[l]), row(ln_ffn_g[l]), row(ln_ffn_b[l]))
    return xs[None]
```

```python
import functools
import math

import jax
import jax.numpy as jnp
import numpy as np
from jax import lax
from jax.experimental import pallas as pl
from jax.experimental.pallas import tpu as pltpu

F32 = jnp.float32
BF16 = jnp.bfloat16

D_MODEL = 2048
SEQ = 8192
DEPTH = 4
CHUNK = 64
N_MEM = 256
LN_EPS = 1e-5
RMS_EPS = 1e-6
ALPHA = (2.0 * DEPTH) ** 0.25

GLA_HEADS = 4
GLA_DV = D_MODEL // 2
GLA_HEAD_V = GLA_DV // GLA_HEADS
GLA_DK = GLA_DV // 2
GLA_HEAD_K = GLA_DK // GLA_HEADS
GLA_RANK = 16
GLA_TAU = 16.0
SSD_D_INNER = D_MODEL - GLA_DV
SSD_HEADDIM = 64
SSD_HEADS = SSD_D_INNER // SSD_HEADDIM
SSD_STATE = 128
SSD_GROUPS = 2
SSD_HPG = SSD_HEADS // SSD_GROUPS
SSD_CONV = 4
SSD_BC = SSD_GROUPS * SSD_STATE
SSD_CONV_DIM = SSD_D_INNER + 2 * SSD_BC
SSD_GROUP_W = SSD_D_INNER // SSD_GROUPS
XA_HEADS = 4
XA_HEAD_DIM = D_MODEL // XA_HEADS
D_FF = ((8 * D_MODEL // 3 + 255) // 256) * 256
FFN_CONV = 3

LANES = 128
SUBLANES = 8
QKVOG_W = 2 * GLA_DK + 2 * GLA_DV
SMALL_W = 2 * LANES
D_IN = QKVOG_W + GLA_RANK + SSD_D_INNER + SSD_CONV_DIM + SSD_HEADS
W_IN_PAD = -(-D_IN // LANES) * LANES
W_IN_FLOOR = D_IN // LANES * LANES
Z_END_PAD = -(-(QKVOG_W + GLA_RANK + SSD_D_INNER) // LANES) * LANES
XBC_START_ALIGNED = (QKVOG_W + GLA_RANK + SSD_D_INNER) // LANES * LANES
DT_LANE = D_IN - SSD_HEADS - (W_IN_PAD - LANES)
LOAD_ROWS = 128
VMEM_LIMIT = 56 * 1024 * 1024

NT = (((1,), (1,)), ((), ()))
TN = (((0,), (0,)), ((), ()))


def _dot(a, b):
    return jnp.dot(a, b, preferred_element_type=F32)


def _dotg(a, b, dims):
    return lax.dot_general(a, b, dims, preferred_element_type=F32)


def _split3(v):
    hi = v.astype(BF16)
    r = v - hi.astype(F32)
    mid = r.astype(BF16)
    lo = (r - mid.astype(F32)).astype(BF16)
    return hi, mid, lo


def _exact_left(mat, v):
    hi, mid, lo = _split3(v)
    return _dot(mat, hi) + _dot(mat, mid) + _dot(mat, lo)


def _exact_right(v, mat):
    hi, mid, lo = _split3(v)
    return _dot(hi, mat) + _dot(mid, mat) + _dot(lo, mat)


def _sigmoid(x):
    return 1.0 / (1.0 + jnp.exp(-x))


def _softplus(x):
    return jnp.maximum(x, 0.0) + jnp.log1p(jnp.exp(-jnp.abs(x)))


def _layer_norm(y, g, b):
    mu = jnp.mean(y, axis=-1, keepdims=True)
    d = y - mu
    var = jnp.mean(d * d, axis=-1, keepdims=True)
    return d * lax.rsqrt(var + LN_EPS) * g + b


def _whole(arr):
    nd = arr.ndim
    return pl.BlockSpec(arr.shape, lambda *_: (0,) * nd, pipeline_mode=pl.Buffered(1))


def _params(sem):
    return pltpu.CompilerParams(dimension_semantics=sem, vmem_limit_bytes=VMEM_LIMIT)


def _weight_chunk(layer, n_load, cols):
    return pl.BlockSpec((None, LOAD_ROWS, cols), lambda s: (layer, jnp.minimum(s, n_load - 1), 0))


def _seq_tile(n_load, tm, cols):
    return pl.BlockSpec((tm, cols), lambda s: (jnp.maximum(s - n_load, 0), 0))


def _chunk_rows(step):
    return pl.ds(pl.multiple_of(step * LOAD_ROWS, LOAD_ROWS), LOAD_ROWS)


def _inproj_kernel(x_ref, w_ref, o1_ref, oz_ref, ox_ref, os_ref, wb_ref, *, n_load):
    s = pl.program_id(0)

    @pl.when(s == 0)
    def _():
        wb_ref[:, W_IN_FLOOR:W_IN_PAD] = jnp.zeros((wb_ref.shape[0], W_IN_PAD - W_IN_FLOOR), BF16)

    @pl.when(s < n_load)
    def _():
        rows = _chunk_rows(s)
        wb_ref[rows, 0:W_IN_FLOOR] = w_ref[:, 0:W_IN_FLOOR].astype(BF16)
        wb_ref[rows, W_IN_FLOOR:D_IN] = w_ref[:, W_IN_FLOOR:D_IN].astype(BF16)

    @pl.when(s >= n_load)
    def _():
        xb = x_ref[...].astype(BF16)
        o1_ref[...] = _dot(xb, wb_ref[:, 0:QKVOG_W])
        lo = _dot(xb, wb_ref[:, QKVOG_W:Z_END_PAD])
        os_ref[:, 0:LANES] = lo[:, 0:LANES]
        oz_ref[...] = lo[:, GLA_RANK:GLA_RANK + SSD_D_INNER]
        hi = _dot(xb, wb_ref[:, XBC_START_ALIGNED:W_IN_PAD])
        ox_ref[...] = hi[:, GLA_RANK:GLA_RANK + SSD_CONV_DIM]
        os_ref[:, LANES:2 * LANES] = hi[:, W_IN_PAD - XBC_START_ALIGNED - LANES:W_IN_PAD - XBC_START_ALIGNED]


def _inproj(x, w_in, layer, tm=256):
    s, d = x.shape
    n_load = d // LOAD_ROWS
    widths = (QKVOG_W, SSD_D_INNER, SSD_CONV_DIM, SMALL_W)
    return pl.pallas_call(
        functools.partial(_inproj_kernel, n_load=n_load),
        grid=(n_load + s // tm,),
        in_specs=[_seq_tile(n_load, tm, d), _weight_chunk(layer, n_load, D_IN)],
        out_specs=[_seq_tile(n_load, tm, n) for n in widths],
        out_shape=[jax.ShapeDtypeStruct((s, n), F32) for n in widths],
        scratch_shapes=[pltpu.VMEM((d, W_IN_PAD), BF16)],
        compiler_params=_params(("arbitrary",)),
        name="inproj",
    )(x, w_in)


def _mixer_kernel(qkvog_ref, z_ref, xbc_ref, sm_ref, wg_ref, bg_ref, gnw_ref, cw_ref, cb_ref,
                  dtb_ref, a_ref, dexp_ref, snw_ref, e_ref, tri_ref, trile_ref, diage_ref, bd_ref,
                  out_ref, sgla_ref, sssd_ref, xprev_ref, *, layer):
    c = pl.program_id(0)
    lrow = slice(layer, layer + 1)

    @pl.when(c == 0)
    def _():
        sgla_ref[...] = jnp.zeros_like(sgla_ref)
        sssd_ref[...] = jnp.zeros_like(sssd_ref)
        xprev_ref[...] = jnp.zeros_like(xprev_ref)

    tri = tri_ref[...]
    tri_mask = tri > 0

    alr = sm_ref[:, 0:LANES].astype(BF16)
    gpre = _dot(alr, wg_ref[...]) + bg_ref[lrow, :]
    log_g = -_softplus(-gpre) / GLA_TAU
    b = _exact_left(tri, log_g)
    b_last = b[CHUNK - 1:CHUNK, :]
    q = qkvog_ref[:, 0:GLA_DK]
    k = qkvog_ref[:, GLA_DK:2 * GLA_DK]
    q_dec = ((q * (GLA_HEAD_K ** -0.5)) * jnp.exp(b)).astype(BF16)
    k_inv = (k * jnp.exp(-b)).astype(BF16)
    k_end = (k * jnp.exp(b_last - b)).astype(BF16)
    e_last = jnp.exp(b_last)
    for h in range(GLA_HEADS):
        ks = slice(h * GLA_HEAD_K, (h + 1) * GLA_HEAD_K)
        vs = slice(2 * GLA_DK + h * GLA_HEAD_V, 2 * GLA_DK + (h + 1) * GLA_HEAD_V)
        gs = slice(2 * GLA_DK + GLA_DV + h * GLA_HEAD_V, 2 * GLA_DK + GLA_DV + (h + 1) * GLA_HEAD_V)
        os_ = slice(h * GLA_HEAD_V, (h + 1) * GLA_HEAD_V)
        vh = qkvog_ref[:, vs].astype(BF16)
        scores = jnp.where(tri_mask, _dotg(q_dec[:, ks], k_inv[:, ks], NT), 0.0)
        state_t = sgla_ref[h]
        o = _dot(scores.astype(BF16), vh) + _dotg(q_dec[:, ks], state_t.astype(BF16), NT)
        sgla_ref[h] = e_last[:, ks] * state_t + _dotg(vh, k_end[:, ks], TN)
        o = o * lax.rsqrt(jnp.mean(o * o, axis=-1, keepdims=True) + RMS_EPS) * gnw_ref[lrow, os_]
        og = qkvog_ref[:, gs]
        out_ref[:, os_] = (o * (og * _sigmoid(og))).astype(BF16)

    xbc = xbc_ref[...]
    xext = jnp.concatenate([xprev_ref[...], xbc], axis=0)
    xprev_ref[...] = xbc[CHUNK - SUBLANES:CHUNK, :]
    conv = cb_ref[lrow, :] + cw_ref[SSD_CONV - 1:SSD_CONV, :] * xbc
    for back in range(1, SSD_CONV):
        shifted = pltpu.roll(xext, back, axis=0)[SUBLANES:SUBLANES + CHUNK, :]
        conv = conv + cw_ref[SSD_CONV - 1 - back:SSD_CONV - back, :] * shifted
    xc = conv * _sigmoid(conv)
    xs = xc[:, 0:SSD_D_INNER]
    dt = _softplus(sm_ref[:, LANES:2 * LANES] + dtb_ref[lrow, :])
    cs = _exact_left(tri, dt * -jnp.exp(a_ref[lrow, :]))
    expand = e_ref[...]
    dt_e = _exact_right(dt, expand)
    cs_e = _exact_right(cs, expand)
    cs_last_e = cs_e[CHUNK - 1:CHUNK, :]
    xdt = xs * dt_e
    cs_row = jnp.sum(cs_e * diage_ref[...], axis=0, keepdims=True)
    lmat = jnp.where(trile_ref[...] > 0, jnp.exp(cs_e - cs_row), 0.0)
    xdt_end = (xdt * jnp.exp(cs_last_e - cs_e)).astype(BF16)
    xdt_b = xdt.astype(BF16)
    decay_in = jnp.exp(cs_e)
    decay_chunk = jnp.exp(cs_last_e)
    bd = bd_ref[...] > 0
    for g in range(SSD_GROUPS):
        gl = slice(g * SSD_GROUP_W, (g + 1) * SSD_GROUP_W)
        bm = xc[:, SSD_D_INNER + g * SSD_STATE:SSD_D_INNER + (g + 1) * SSD_STATE].astype(BF16)
        cm = xc[:, SSD_D_INNER + SSD_BC + g * SSD_STATE:
                SSD_D_INNER + SSD_BC + (g + 1) * SSD_STATE].astype(BF16)
        bm_rep = jnp.concatenate([bm] * SSD_HPG, axis=0)
        w = (_dotg(cm, bm_rep, NT) * lmat[:, gl]).astype(BF16)
        state_t = sssd_ref[g]
        y_off = _dot(cm, state_t.astype(BF16)) * decay_in[:, gl]
        sssd_ref[g] = decay_chunk[:, gl] * state_t + _dotg(bm, xdt_end[:, gl], TN)
        y_diag = []
        for hp in range(SSD_HPG // 2):
            pl_ = slice(g * SSD_GROUP_W + hp * 2 * SSD_HEADDIM, g * SSD_GROUP_W + (hp + 1) * 2 * SSD_HEADDIM)
            wl = slice(hp * 2 * CHUNK, (hp + 1) * 2 * CHUNK)
            x2 = xdt_b[:, pl_]
            x_bd = jnp.where(bd, jnp.concatenate([x2, x2], axis=0), jnp.zeros((), BF16))
            y_diag.append(_dot(w[:, wl], x_bd))
        y = jnp.concatenate(y_diag, axis=1) + y_off + dexp_ref[lrow, gl] * xs[:, gl]
        zg = z_ref[:, gl]
        y = y * (zg * _sigmoid(zg))
        y = y * lax.rsqrt(jnp.mean(y * y, axis=-1, keepdims=True) + RMS_EPS) * snw_ref[lrow, gl]
        out_ref[:, GLA_DV + g * SSD_GROUP_W:GLA_DV + (g + 1) * SSD_GROUP_W] = y.astype(BF16)


def _mixer_constants():
    l = np.arange(CHUNK)
    tri = (l[None, :] <= l[:, None]).astype(np.float32)
    lane = np.arange(SSD_D_INNER)
    m_of_lane = lane % SSD_HEADDIM
    trile = (m_of_lane[None, :] <= l[:, None]).astype(np.float32)
    diage = (m_of_lane[None, :] == l[:, None]).astype(np.float32)
    expand = np.zeros((LANES, SSD_D_INNER), np.float32)
    expand[DT_LANE + lane // SSD_HEADDIM, lane] = 1.0
    r = np.arange(2 * SSD_HEADDIM)
    bd = ((r[:, None] // SSD_HEADDIM) == (r[None, :] // SSD_HEADDIM)).astype(np.float32)
    return (jnp.asarray(expand, BF16), jnp.asarray(tri, BF16), jnp.asarray(trile), jnp.asarray(diage),
            jnp.asarray(bd))


def _mixer(qkvog, z, xbc, sm, wg_all, per_layer, layer):
    assert CHUNK == SSD_HEADDIM
    s = qkvog.shape[0]
    consts = _mixer_constants()
    row = lambda n: pl.BlockSpec((CHUNK, n), lambda i: (i, 0))
    bg, gnw, cw, cb, dtb, a_log, dexp, snw = per_layer
    layer_slab = lambda t: pl.BlockSpec((None,) + t.shape[1:], lambda i: (layer, 0, 0))
    order = (wg_all, bg, gnw, cw, cb, dtb, a_log, dexp, snw) + consts
    in_specs = [row(QKVOG_W), row(SSD_D_INNER), row(SSD_CONV_DIM), row(SMALL_W),
                layer_slab(wg_all), _whole(bg), _whole(gnw), layer_slab(cw)]
    in_specs += [_whole(t) for t in order[4:]]
    return pl.pallas_call(
        functools.partial(_mixer_kernel, layer=layer),
        grid=(s // CHUNK,),
        in_specs=in_specs,
        out_specs=row(D_MODEL),
        out_shape=jax.ShapeDtypeStruct((s, D_MODEL), BF16),
        scratch_shapes=[
            pltpu.VMEM((GLA_HEADS, GLA_HEAD_V, GLA_HEAD_K), F32),
            pltpu.VMEM((SSD_GROUPS, SSD_STATE, SSD_GROUP_W), F32),
            pltpu.VMEM((SUBLANES, SSD_CONV_DIM), F32),
        ],
        compiler_params=_params(("arbitrary",)),
        name="mixer",
    )(qkvog, z, xbc, sm, *order)


def _proj_ln_kernel(a_ref, w_ref, x_ref, g_ref, b_ref, o_ref, wb_ref, *, n_load, layer):
    s = pl.program_id(0)
    lrow = slice(layer, layer + 1)

    @pl.when(s < n_load)
    def _():
        wb_ref[_chunk_rows(s), :] = w_ref[...].astype(BF16)

    @pl.when(s >= n_load)
    def _():
        h = _dot(a_ref[...], wb_ref[...])
        o_ref[...] = _layer_norm(ALPHA * x_ref[...] + h, g_ref[lrow, :], b_ref[lrow, :])


def _proj_ln(a, w_all, x, g_all, b_all, layer, tm=512):
    s, d = x.shape
    kdim = a.shape[1]
    n_load = kdim // LOAD_ROWS
    return pl.pallas_call(
        functools.partial(_proj_ln_kernel, n_load=n_load, layer=layer),
        grid=(n_load + s // tm,),
        in_specs=[_seq_tile(n_load, tm, kdim), _weight_chunk(layer, n_load, d), _seq_tile(n_load, tm, d),
                  _whole(g_all), _whole(b_all)],
        out_specs=_seq_tile(n_load, tm, d),
        out_shape=jax.ShapeDtypeStruct((s, d), F32),
        scratch_shapes=[pltpu.VMEM((kdim, d), BF16)],
        compiler_params=_params(("arbitrary",)),
        name="proj_ln",
    )(a, w_all, x, g_all, b_all)


def _xattn_kernel(x_ref, mem_ref, wq_ref, wk_ref, wv_ref, wo_ref, g_ref, b_ref, o_ref,
                  wqb_ref, wob_ref, kacc_ref, vacc_ref, kb_ref, vb_ref, att_ref, *, n_load, layer):
    s = pl.program_id(0)
    lrow = slice(layer, layer + 1)

    @pl.when(s == 0)
    def _():
        kacc_ref[...] = jnp.zeros_like(kacc_ref)
        vacc_ref[...] = jnp.zeros_like(vacc_ref)

    @pl.when(s < n_load)
    def _():
        rows = _chunk_rows(s)
        wqb_ref[rows, :] = wq_ref[...].astype(BF16)
        wob_ref[rows, :] = wo_ref[...].astype(BF16)
        mb = mem_ref[...].astype(BF16)
        kacc_ref[...] += _dot(mb, wk_ref[...].astype(BF16))
        vacc_ref[...] += _dot(mb, wv_ref[...].astype(BF16))

    @pl.when(s == n_load - 1)
    def _():
        kb_ref[...] = kacc_ref[...].astype(BF16)
        vb_ref[...] = vacc_ref[...].astype(BF16)

    @pl.when(s >= n_load)
    def _():
        x = x_ref[...]
        q = _dot(x.astype(BF16), wqb_ref[...])
        for h in range(XA_HEADS):
            hs = slice(h * XA_HEAD_DIM, (h + 1) * XA_HEAD_DIM)
            sc = _dotg(q[:, hs].astype(BF16), kb_ref[:, hs], NT) * (XA_HEAD_DIM ** -0.5)
            e = jnp.exp(sc - jnp.max(sc, axis=-1, keepdims=True))
            p = e / jnp.sum(e, axis=-1, keepdims=True)
            att_ref[:, hs] = _dot(p.astype(BF16), vb_ref[:, hs]).astype(BF16)
        h_out = _dot(att_ref[...], wob_ref[...])
        o_ref[...] = _layer_norm(ALPHA * x + h_out, g_ref[lrow, :], b_ref[lrow, :])


def _xattn(x, mem, wq_all, wk_all, wv_all, wo_all, g_all, b_all, layer, tm=512):
    s, d = x.shape
    m = mem.shape[1]
    n_load = d // LOAD_ROWS
    mem_spec = pl.BlockSpec((None, m, LOAD_ROWS), lambda t: (0, 0, jnp.minimum(t, n_load - 1)))
    wchunk = _weight_chunk(layer, n_load, d)
    return pl.pallas_call(
        functools.partial(_xattn_kernel, n_load=n_load, layer=layer),
        grid=(n_load + s // tm,),
        in_specs=[_seq_tile(n_load, tm, d), mem_spec, wchunk, wchunk, wchunk, wchunk,
                  _whole(g_all), _whole(b_all)],
        out_specs=_seq_tile(n_load, tm, d),
        out_shape=jax.ShapeDtypeStruct((s, d), F32),
        scratch_shapes=[pltpu.VMEM((d, d), BF16), pltpu.VMEM((d, d), BF16),
                        pltpu.VMEM((m, d), F32), pltpu.VMEM((m, d), F32),
                        pltpu.VMEM((m, d), BF16), pltpu.VMEM((m, d), BF16),
                        pltpu.VMEM((tm, d), BF16)],
        compiler_params=_params(("arbitrary",)),
        name="xattn",
    )(x, mem, wq_all, wk_all, wv_all, wo_all, g_all, b_all)


def _ffn_kernel(x_ref, wg_ref, wv_ref, wd_ref, cw_ref, cb_ref, g_ref, b_ref, o_ref,
                xb_ref, acc_ref, carry_ref, gext_ref, *, layer):
    i = pl.program_id(0)
    j = pl.program_id(1)
    tm = x_ref.shape[0]
    lrow = slice(layer, layer + 1)

    @pl.when(j == 0)
    def _():
        xb_ref[...] = x_ref[...].astype(BF16)
        acc_ref[...] = jnp.zeros_like(acc_ref)

    @pl.when(i == 0)
    def _():
        carry_ref[j] = jnp.zeros(carry_ref.shape[1:], F32)

    xb = xb_ref[...]
    gate = _dot(xb, wg_ref[...])
    val = _dot(xb, wv_ref[...])
    gext_ref[0:SUBLANES, :] = carry_ref[j]
    gext_ref[SUBLANES:SUBLANES + tm, :] = gate
    carry_ref[j] = gate[tm - SUBLANES:tm, :]
    conv = cb_ref[lrow, :] + cw_ref[FFN_CONV - 1:FFN_CONV, :] * gate
    for t in range(FFN_CONV - 1):
        off = SUBLANES - (FFN_CONV - 1) + t
        conv = conv + cw_ref[t:t + 1, :] * gext_ref[off:off + tm, :]
    act = 0.5 * conv * (1.0 + lax.erf(conv * math.sqrt(0.5)))
    acc_ref[...] += _dot((act * val).astype(BF16), wd_ref[...])

    @pl.when(j == pl.num_programs(1) - 1)
    def _():
        o_ref[...] = _layer_norm(ALPHA * x_ref[...] + acc_ref[...], g_ref[lrow, :], b_ref[lrow, :])


def _ffn(x, wup_all, wd_all, cw_all, cb_all, g_all, b_all, layer, tm=512, tf=512):
    s, d = x.shape
    f = wd_all.shape[1]
    nf = f // tf
    nl = cb_all.shape[0]
    return pl.pallas_call(
        functools.partial(_ffn_kernel, layer=layer),
        grid=(s // tm, nf),
        in_specs=[pl.BlockSpec((tm, d), lambda i, j: (i, 0)),
                  pl.BlockSpec((None, d, tf), lambda i, j: (layer, 0, j)),
                  pl.BlockSpec((None, d, tf), lambda i, j: (layer, 0, j + nf)),
                  pl.BlockSpec((None, tf, d), lambda i, j: (layer, j, 0)),
                  pl.BlockSpec((None, FFN_CONV, tf), lambda i, j: (layer, 0, j)),
                  pl.BlockSpec((nl, tf), lambda i, j: (0, j)),
                  pl.BlockSpec((nl, d), lambda i, j: (0, 0)),
                  pl.BlockSpec((nl, d), lambda i, j: (0, 0))],
        out_specs=pl.BlockSpec((tm, d), lambda i, j: (i, 0)),
        out_shape=jax.ShapeDtypeStruct((s, d), F32),
        scratch_shapes=[pltpu.VMEM((tm, d), BF16), pltpu.VMEM((tm, d), F32),
                        pltpu.VMEM((nf, SUBLANES, tf), F32), pltpu.VMEM((SUBLANES + tm, tf), F32)],
        compiler_params=_params(("arbitrary", "arbitrary")),
        name="ffn",
    )(x, wup_all, wup_all, wd_all, cw_all, cb_all, g_all, b_all)


def _at_lanes(table, start):
    return jnp.pad(table, ((0, 0), (start, LANES - start - table.shape[1])))


def kernel(x, mem, w_in, gla_w_gate, gla_b_gate, gla_norm_w, ssd_conv_w, ssd_conv_b, ssd_dt_bias, ssd_a_log, ssd_d, ssd_norm_w, w_out, ln_mix_g, ln_mix_b, xa_wq, xa_wk, xa_wv, xa_wo, ln_xa_g, ln_xa_b, ffn_w_up, ffn_conv_w, ffn_conv_b, ffn_w_down, ln_ffn_g, ln_ffn_b):
    assert x.shape == (1, SEQ, D_MODEL) and mem.shape == (1, N_MEM, D_MODEL)
    assert w_in.shape == (DEPTH, D_MODEL, D_IN)
    xs = x[0]
    wg_all = jnp.pad(gla_w_gate, ((0, 0), (0, LANES - GLA_RANK), (0, 0))).astype(BF16)
    mixer_tables = (gla_b_gate, gla_norm_w, ssd_conv_w, ssd_conv_b, _at_lanes(ssd_dt_bias, DT_LANE),
                    _at_lanes(ssd_a_log, DT_LANE), jnp.repeat(ssd_d, SSD_HEADDIM, axis=1), ssd_norm_w)
    wup_all = ffn_w_up.astype(BF16)
    wdn_all = ffn_w_down.astype(BF16)
    for l in range(DEPTH):
        qkvog, z, xbc, sm = _inproj(xs, w_in, l)
        mixed = _mixer(qkvog, z, xbc, sm, wg_all, mixer_tables, l)
        xs = _proj_ln(mixed, w_out, xs, ln_mix_g, ln_mix_b, l)
        xs = _xattn(xs, mem, xa_wq, xa_wk, xa_wv, xa_wo, ln_xa_g, ln_xa_b, l)
        xs = _ffn(xs, wup_all, wdn_all, ffn_conv_w, ffn_conv_b, ln_ffn_g, ln_ffn_b, l)
    return xs[None]
```

```python
import functools
import math

import jax
import jax.numpy as jnp
import numpy as np
from jax import lax
from jax.experimental import pallas as pl
from jax.experimental.pallas import tpu as pltpu

F32 = jnp.float32
BF16 = jnp.bfloat16

D_MODEL = 2048
SEQ = 8192
DEPTH = 4
CHUNK = 64
N_MEM = 256
LN_EPS = 1e-5
RMS_EPS = 1e-6
ALPHA = (2.0 * DEPTH) ** 0.25

GLA_HEADS = 4
GLA_DV = D_MODEL // 2
GLA_HEAD_V = GLA_DV // GLA_HEADS
GLA_DK = GLA_DV // 2
GLA_HEAD_K = GLA_DK // GLA_HEADS
GLA_RANK = 16
GLA_TAU = 16.0
SSD_D_INNER = D_MODEL - GLA_DV
SSD_HEADDIM = 64
SSD_HEADS = SSD_D_INNER // SSD_HEADDIM
SSD_STATE = 128
SSD_GROUPS = 2
SSD_HPG = SSD_HEADS // SSD_GROUPS
SSD_CONV = 4
SSD_BC = SSD_GROUPS * SSD_STATE
SSD_CONV_DIM = SSD_D_INNER + 2 * SSD_BC
SSD_GROUP_W = SSD_D_INNER // SSD_GROUPS
XA_HEADS = 4
XA_HEAD_DIM = D_MODEL // XA_HEADS
D_FF = ((8 * D_MODEL // 3 + 255) // 256) * 256
FFN_CONV = 3

LANES = 128
SUBLANES = 8
QKVOG_W = 2 * GLA_DK + 2 * GLA_DV
SMALL_W = 2 * LANES
D_IN = QKVOG_W + GLA_RANK + SSD_D_INNER + SSD_CONV_DIM + SSD_HEADS
W_IN_PAD = -(-D_IN // LANES) * LANES
W_IN_FLOOR = D_IN // LANES * LANES
Z_END_PAD = -(-(QKVOG_W + GLA_RANK + SSD_D_INNER) // LANES) * LANES
XBC_START_ALIGNED = (QKVOG_W + GLA_RANK + SSD_D_INNER) // LANES * LANES
DT_LANE = D_IN - SSD_HEADS - (W_IN_PAD - LANES)
LOAD_ROWS = 128
INPROJ_LOAD_ROWS = 256
ROW_GROUPS = 4
XA_ROW_GROUPS = 1
VMEM_LIMIT = 56 * 1024 * 1024

NT = (((1,), (1,)), ((), ()))
TN = (((0,), (0,)), ((), ()))


def _dot(a, b):
    return jnp.dot(a, b, preferred_element_type=F32)


def _dotg(a, b, dims):
    return lax.dot_general(a, b, dims, preferred_element_type=F32)


def _split3(v):
    hi = v.astype(BF16)
    r = v - hi.astype(F32)
    mid = r.astype(BF16)
    lo = (r - mid.astype(F32)).astype(BF16)
    return hi, mid, lo


def _exact_left(mat, v):
    hi, mid, lo = _split3(v)
    return _dot(mat, hi) + _dot(mat, mid) + _dot(mat, lo)


def _exact_right(v, mat):
    hi, mid, lo = _split3(v)
    return _dot(hi, mat) + _dot(mid, mat) + _dot(lo, mat)


def _sigmoid(x):
    return 1.0 / (1.0 + jnp.exp(-x))


def _softplus(x):
    return jnp.maximum(x, 0.0) + jnp.log1p(jnp.exp(-jnp.abs(x)))


def _layer_norm(y, g, b):
    mu = jnp.mean(y, axis=-1, keepdims=True)
    d = y - mu
    var = jnp.mean(d * d, axis=-1, keepdims=True)
    return d * lax.rsqrt(var + LN_EPS) * g + b


def _whole(arr):
    nd = arr.ndim
    return pl.BlockSpec(arr.shape, lambda *_: (0,) * nd, pipeline_mode=pl.Buffered(1))


def _params(sem):
    return pltpu.CompilerParams(dimension_semantics=sem, vmem_limit_bytes=VMEM_LIMIT)


def _weight_chunk(layer, n_load, cols):
    return pl.BlockSpec((None, LOAD_ROWS, cols), lambda s: (layer, jnp.minimum(s, n_load - 1), 0))


def _seq_tile(n_load, tm, cols):
    return pl.BlockSpec((tm, cols), lambda s: (jnp.maximum(s - n_load, 0), 0))


def _chunk_rows(step):
    return pl.ds(pl.multiple_of(step * LOAD_ROWS, LOAD_ROWS), LOAD_ROWS)


def _inproj_kernel(x_ref, wt_ref, o1_ref, oz_ref, ox_ref, os_ref, wb_ref, *, n_load):
    s = pl.program_id(0)
    full_rows = (n_load - 1) * INPROJ_LOAD_ROWS
    last_rows = D_IN - full_rows

    @pl.when(s < n_load - 1)
    def _():
        rows = pl.ds(pl.multiple_of(s * INPROJ_LOAD_ROWS, INPROJ_LOAD_ROWS), INPROJ_LOAD_ROWS)
        wb_ref[rows, :] = wt_ref[...].astype(BF16)

    @pl.when(s == n_load - 1)
    def _():
        wb_ref[full_rows:D_IN, :] = wt_ref[0:last_rows, :].astype(BF16)
        wb_ref[D_IN:W_IN_PAD, :] = jnp.zeros((W_IN_PAD - D_IN, wb_ref.shape[1]), BF16)

    @pl.when(s >= n_load)
    def _():
        xb = x_ref[...].astype(BF16)
        o1_ref[...] = _dotg(xb, wb_ref[0:QKVOG_W, :], NT)
        lo = _dotg(xb, wb_ref[QKVOG_W:Z_END_PAD, :], NT)
        os_ref[:, 0:LANES] = lo[:, 0:LANES]
        oz_ref[...] = lo[:, GLA_RANK:GLA_RANK + SSD_D_INNER]
        hi = _dotg(xb, wb_ref[XBC_START_ALIGNED:W_IN_PAD, :], NT)
        ox_ref[...] = hi[:, GLA_RANK:GLA_RANK + SSD_CONV_DIM]
        os_ref[:, LANES:2 * LANES] = hi[:, W_IN_PAD - XBC_START_ALIGNED - LANES:W_IN_PAD - XBC_START_ALIGNED]


def _inproj(x, w_in_t, layer, tm=256):
    s, d = x.shape
    n_load = -(-D_IN // INPROJ_LOAD_ROWS)
    assert (D_IN - (n_load - 1) * INPROJ_LOAD_ROWS) % (2 * SUBLANES) == 0
    widths = (QKVOG_W, SSD_D_INNER, SSD_CONV_DIM, SMALL_W)
    wt_spec = pl.BlockSpec((None, INPROJ_LOAD_ROWS, d), lambda t: (layer, jnp.minimum(t, n_load - 1), 0))
    return pl.pallas_call(
        functools.partial(_inproj_kernel, n_load=n_load),
        grid=(n_load + s // tm,),
        in_specs=[_seq_tile(n_load, tm, d), wt_spec],
        out_specs=[_seq_tile(n_load, tm, n) for n in widths],
        out_shape=[jax.ShapeDtypeStruct((s, n), F32) for n in widths],
        scratch_shapes=[pltpu.VMEM((W_IN_PAD, d), BF16)],
        compiler_params=_params(("arbitrary",)),
        name="inproj",
    )(x, w_in_t)


def _mixer_kernel(qkvog_ref, z_ref, xbc_ref, sm_ref, wg_ref, bg_ref, gnw_ref, cw_ref, cb_ref,
                  dtb_ref, a_ref, dexp_ref, snw_ref, e_ref, tri_ref, trile_ref, diage_ref, bd_ref,
                  wup_ref, wdn_ref, out_ref, wup_b_ref, wdn_b_ref, sgla_ref, sssd_ref, xprev_ref, *, layer):
    c = pl.program_id(0)
    lrow = slice(layer, layer + 1)
    wup_b_ref[...] = wup_ref[...].astype(BF16)
    wdn_b_ref[...] = wdn_ref[...].astype(BF16)

    @pl.when(c == 0)
    def _():
        sgla_ref[...] = jnp.zeros_like(sgla_ref)
        sssd_ref[...] = jnp.zeros_like(sssd_ref)
        xprev_ref[...] = jnp.zeros_like(xprev_ref)

    refs = (qkvog_ref, z_ref, xbc_ref, sm_ref, wg_ref, bg_ref, gnw_ref, cw_ref, cb_ref, dtb_ref, a_ref,
            dexp_ref, snw_ref, e_ref, tri_ref, trile_ref, diage_ref, bd_ref, out_ref, sgla_ref, sssd_ref)
    tail = xprev_ref[...]
    for ci in range(qkvog_ref.shape[0] // CHUNK):
        rows = slice(ci * CHUNK, (ci + 1) * CHUNK)
        _mixer_chunk(refs, rows, tail, lrow)
        tail = xbc_ref[(ci + 1) * CHUNK - SUBLANES:(ci + 1) * CHUNK, :]
    xprev_ref[...] = tail


def _mixer_chunk(refs, rows, tail, lrow):
    (qkvog_ref, z_ref, xbc_ref, sm_ref, wg_ref, bg_ref, gnw_ref, cw_ref, cb_ref, dtb_ref, a_ref,
     dexp_ref, snw_ref, e_ref, tri_ref, trile_ref, diage_ref, bd_ref, out_ref, sgla_ref, sssd_ref) = refs
    tri = tri_ref[...]
    tri_mask = tri > 0

    alr = sm_ref[rows, 0:LANES].astype(BF16)
    gpre = _dot(alr, wg_ref[...]) + bg_ref[lrow, :]
    log_g = -_softplus(-gpre) / GLA_TAU
    b = _exact_left(tri, log_g)
    b_last = b[CHUNK - 1:CHUNK, :]
    q = qkvog_ref[rows, 0:GLA_DK]
    k = qkvog_ref[rows, GLA_DK:2 * GLA_DK]
    q_dec = ((q * (GLA_HEAD_K ** -0.5)) * jnp.exp(b)).astype(BF16)
    k_inv = (k * jnp.exp(-b)).astype(BF16)
    k_end = (k * jnp.exp(b_last - b)).astype(BF16)
    e_last = jnp.exp(b_last)
    for h in range(GLA_HEADS):
        ks = slice(h * GLA_HEAD_K, (h + 1) * GLA_HEAD_K)
        vs = slice(2 * GLA_DK + h * GLA_HEAD_V, 2 * GLA_DK + (h + 1) * GLA_HEAD_V)
        gs = slice(2 * GLA_DK + GLA_DV + h * GLA_HEAD_V, 2 * GLA_DK + GLA_DV + (h + 1) * GLA_HEAD_V)
        os_ = slice(h * GLA_HEAD_V, (h + 1) * GLA_HEAD_V)
        vh = qkvog_ref[rows, vs].astype(BF16)
        scores = jnp.where(tri_mask, _dotg(q_dec[:, ks], k_inv[:, ks], NT), 0.0)
        state_t = sgla_ref[h]
        o = _dot(scores.astype(BF16), vh) + _dotg(q_dec[:, ks], state_t.astype(BF16), NT)
        sgla_ref[h] = e_last[:, ks] * state_t + _dotg(vh, k_end[:, ks], TN)
        o = o * lax.rsqrt(jnp.mean(o * o, axis=-1, keepdims=True) + RMS_EPS) * gnw_ref[lrow, os_]
        og = qkvog_ref[rows, gs]
        out_ref[rows, os_] = (o * (og * _sigmoid(og))).astype(BF16)

    xbc = xbc_ref[rows, :]
    xext = jnp.concatenate([tail, xbc], axis=0)
    conv = cb_ref[lrow, :] + cw_ref[SSD_CONV - 1:SSD_CONV, :] * xbc
    for back in range(1, SSD_CONV):
        shifted = pltpu.roll(xext, back, axis=0)[SUBLANES:SUBLANES + CHUNK, :]
        conv = conv + cw_ref[SSD_CONV - 1 - back:SSD_CONV - back, :] * shifted
    xc = conv * _sigmoid(conv)
    xs = xc[:, 0:SSD_D_INNER]
    dt = _softplus(sm_ref[rows, LANES:2 * LANES] + dtb_ref[lrow, :])
    cs = _exact_left(tri, dt * -jnp.exp(a_ref[lrow, :]))
    expand = e_ref[...]
    dt_e = _exact_right(dt, expand)
    cs_e = _exact_right(cs, expand)
    cs_last_e = cs_e[CHUNK - 1:CHUNK, :]
    xdt = xs * dt_e
    cs_row = jnp.sum(cs_e * diage_ref[...], axis=0, keepdims=True)
    lmat = jnp.where(trile_ref[...] > 0, jnp.exp(cs_e - cs_row), 0.0)
    xdt_end = (xdt * jnp.exp(cs_last_e - cs_e)).astype(BF16)
    xdt_b = xdt.astype(BF16)
    decay_in = jnp.exp(cs_e)
    decay_chunk = jnp.exp(cs_last_e)
    bd = bd_ref[...] > 0
    for g in range(SSD_GROUPS):
        gl = slice(g * SSD_GROUP_W, (g + 1) * SSD_GROUP_W)
        bm = xc[:, SSD_D_INNER + g * SSD_STATE:SSD_D_INNER + (g + 1) * SSD_STATE].astype(BF16)
        cm = xc[:, SSD_D_INNER + SSD_BC + g * SSD_STATE:
                SSD_D_INNER + SSD_BC + (g + 1) * SSD_STATE].astype(BF16)
        bm_rep = jnp.concatenate([bm] * SSD_HPG, axis=0)
        w = (_dotg(cm, bm_rep, NT) * lmat[:, gl]).astype(BF16)
        state_t = sssd_ref[g]
        y_off = _dot(cm, state_t.astype(BF16)) * decay_in[:, gl]
        sssd_ref[g] = decay_chunk[:, gl] * state_t + _dotg(bm, xdt_end[:, gl], TN)
        y_diag = []
        for hp in range(SSD_HPG // 2):
            pl_ = slice(g * SSD_GROUP_W + hp * 2 * SSD_HEADDIM, g * SSD_GROUP_W + (hp + 1) * 2 * SSD_HEADDIM)
            wl = slice(hp * 2 * CHUNK, (hp + 1) * 2 * CHUNK)
            x2 = xdt_b[:, pl_]
            x_bd = jnp.where(bd, jnp.concatenate([x2, x2], axis=0), jnp.zeros((), BF16))
            y_diag.append(_dot(w[:, wl], x_bd))
        y = jnp.concatenate(y_diag, axis=1) + y_off + dexp_ref[lrow, gl] * xs[:, gl]
        zg = z_ref[rows, gl]
        y = y * (zg * _sigmoid(zg))
        y = y * lax.rsqrt(jnp.mean(y * y, axis=-1, keepdims=True) + RMS_EPS) * snw_ref[lrow, gl]
        out_ref[rows, GLA_DV + g * SSD_GROUP_W:GLA_DV + (g + 1) * SSD_GROUP_W] = y.astype(BF16)


def _mixer_constants():
    l = np.arange(CHUNK)
    tri = (l[None, :] <= l[:, None]).astype(np.float32)
    lane = np.arange(SSD_D_INNER)
    m_of_lane = lane % SSD_HEADDIM
    trile = (m_of_lane[None, :] <= l[:, None]).astype(np.float32)
    diage = (m_of_lane[None, :] == l[:, None]).astype(np.float32)
    expand = np.zeros((LANES, SSD_D_INNER), np.float32)
    expand[DT_LANE + lane // SSD_HEADDIM, lane] = 1.0
    r = np.arange(2 * SSD_HEADDIM)
    bd = ((r[:, None] // SSD_HEADDIM) == (r[None, :] // SSD_HEADDIM)).astype(np.float32)
    return (jnp.asarray(expand, BF16), jnp.asarray(tri, BF16), jnp.asarray(trile), jnp.asarray(diage),
            jnp.asarray(bd))


def _mixer(qkvog, z, xbc, sm, wg_all, per_layer, ffn_w_up, ffn_w_down, layer, chunks_per_step=4):
    assert CHUNK == SSD_HEADDIM
    s = qkvog.shape[0]
    consts = _mixer_constants()
    tm = chunks_per_step * CHUNK
    steps = s // tm
    row = lambda n: pl.BlockSpec((tm, n), lambda i: (i, 0))
    bg, gnw, cw, cb, dtb, a_log, dexp, snw = per_layer
    layer_slab = lambda t: pl.BlockSpec((None,) + t.shape[1:], lambda i: (layer, 0, 0))
    order = (wg_all, bg, gnw, cw, cb, dtb, a_log, dexp, snw) + consts
    in_specs = [row(QKVOG_W), row(SSD_D_INNER), row(SSD_CONV_DIM), row(SMALL_W),
                layer_slab(wg_all), _whole(bg), _whole(gnw), layer_slab(cw)]
    in_specs += [_whole(t) for t in order[4:]]
    up_rows, up_cols = ffn_w_up.shape[1] // steps, ffn_w_up.shape[2]
    dn_rows, dn_cols = ffn_w_down.shape[1] // steps, ffn_w_down.shape[2]
    assert up_rows * steps == ffn_w_up.shape[1] and up_rows % (2 * SUBLANES) == 0
    assert dn_rows * steps == ffn_w_down.shape[1] and dn_rows % (2 * SUBLANES) == 0
    in_specs += [pl.BlockSpec((None, up_rows, up_cols), lambda i: (layer, i, 0)),
                 pl.BlockSpec((None, dn_rows, dn_cols), lambda i: (layer, i, 0))]
    return pl.pallas_call(
        functools.partial(_mixer_kernel, layer=layer),
        grid=(steps,),
        in_specs=in_specs,
        out_specs=[row(D_MODEL), pl.BlockSpec((up_rows, up_cols), lambda i: (i, 0)),
                   pl.BlockSpec((dn_rows, dn_cols), lambda i: (i, 0))],
        out_shape=[jax.ShapeDtypeStruct((s, D_MODEL), BF16),
                   jax.ShapeDtypeStruct(ffn_w_up.shape[1:], BF16),
                   jax.ShapeDtypeStruct(ffn_w_down.shape[1:], BF16)],
        scratch_shapes=[
            pltpu.VMEM((GLA_HEADS, GLA_HEAD_V, GLA_HEAD_K), F32),
            pltpu.VMEM((SSD_GROUPS, SSD_STATE, SSD_GROUP_W), F32),
            pltpu.VMEM((SUBLANES, SSD_CONV_DIM), F32),
        ],
        compiler_params=_params(("arbitrary",)),
        name="mixer",
    )(qkvog, z, xbc, sm, *order, ffn_w_up, ffn_w_down)


def _proj_ln_kernel(a_ref, w_ref, x_ref, g_ref, b_ref, o_ref, wb_ref, *, n_load, layer):
    s = pl.program_id(0)
    lrow = slice(layer, layer + 1)

    @pl.when(s < n_load)
    def _():
        wb_ref[_chunk_rows(s), :] = w_ref[...].astype(BF16)

    @pl.when(s >= n_load)
    def _():
        half = a_ref.shape[0] // ROW_GROUPS
        for r in range(ROW_GROUPS):
            rows = slice(r * half, (r + 1) * half)
            h = _dot(a_ref[rows, :], wb_ref[...])
            o_ref[rows, :] = _layer_norm(ALPHA * x_ref[rows, :] + h, g_ref[lrow, :], b_ref[lrow, :])


def _proj_ln(a, w_all, x, g_all, b_all, layer, tm=512):
    s, d = x.shape
    kdim = a.shape[1]
    n_load = kdim // LOAD_ROWS
    return pl.pallas_call(
        functools.partial(_proj_ln_kernel, n_load=n_load, layer=layer),
        grid=(n_load + s // tm,),
        in_specs=[_seq_tile(n_load, tm, kdim), _weight_chunk(layer, n_load, d), _seq_tile(n_load, tm, d),
                  _whole(g_all), _whole(b_all)],
        out_specs=_seq_tile(n_load, tm, d),
        out_shape=jax.ShapeDtypeStruct((s, d), F32),
        scratch_shapes=[pltpu.VMEM((kdim, d), BF16)],
        compiler_params=_params(("arbitrary",)),
        name="proj_ln",
    )(a, w_all, x, g_all, b_all)


def _xattn_kernel(x_ref, mem_ref, wq_ref, wk_ref, wv_ref, wo_ref, g_ref, b_ref, o_ref,
                  wqb_ref, wob_ref, kacc_ref, vacc_ref, kb_ref, vb_ref, att_ref, *, n_load, layer):
    s = pl.program_id(0)
    lrow = slice(layer, layer + 1)

    @pl.when(s == 0)
    def _():
        kacc_ref[...] = jnp.zeros_like(kacc_ref)
        vacc_ref[...] = jnp.zeros_like(vacc_ref)

    @pl.when(s < n_load)
    def _():
        rows = _chunk_rows(s)
        wqb_ref[rows, :] = wq_ref[...].astype(BF16)
        wob_ref[rows, :] = wo_ref[...].astype(BF16)
        mb = mem_ref[...].astype(BF16)
        kacc_ref[...] += _dot(mb, wk_ref[...].astype(BF16))
        vacc_ref[...] += _dot(mb, wv_ref[...].astype(BF16))

    @pl.when(s == n_load - 1)
    def _():
        kb_ref[...] = kacc_ref[...].astype(BF16)
        vb_ref[...] = vacc_ref[...].astype(BF16)

    @pl.when(s >= n_load)
    def _():
        group = x_ref.shape[0] // XA_ROW_GROUPS
        for r in range(XA_ROW_GROUPS):
            rows = slice(r * group, (r + 1) * group)
            x = x_ref[rows, :]
            q = _dot(x.astype(BF16), wqb_ref[...])
            for h in range(XA_HEADS):
                hs = slice(h * XA_HEAD_DIM, (h + 1) * XA_HEAD_DIM)
                sc = _dotg(q[:, hs].astype(BF16), kb_ref[:, hs], NT) * (XA_HEAD_DIM ** -0.5)
                e = jnp.exp(sc - jnp.max(sc, axis=-1, keepdims=True))
                p = e / jnp.sum(e, axis=-1, keepdims=True)
                att_ref[rows, hs] = _dot(p.astype(BF16), vb_ref[:, hs]).astype(BF16)
            h_out = _dot(att_ref[rows, :], wob_ref[...])
            o_ref[rows, :] = _layer_norm(ALPHA * x + h_out, g_ref[lrow, :], b_ref[lrow, :])


def _xattn(x, mem, wq_all, wk_all, wv_all, wo_all, g_all, b_all, layer, tm=512):
    s, d = x.shape
    m = mem.shape[1]
    n_load = d // LOAD_ROWS
    mem_spec = pl.BlockSpec((None, m, LOAD_ROWS), lambda t: (0, 0, jnp.minimum(t, n_load - 1)))
    wchunk = _weight_chunk(layer, n_load, d)
    return pl.pallas_call(
        functools.partial(_xattn_kernel, n_load=n_load, layer=layer),
        grid=(n_load + s // tm,),
        in_specs=[_seq_tile(n_load, tm, d), mem_spec, wchunk, wchunk, wchunk, wchunk,
                  _whole(g_all), _whole(b_all)],
        out_specs=_seq_tile(n_load, tm, d),
        out_shape=jax.ShapeDtypeStruct((s, d), F32),
        scratch_shapes=[pltpu.VMEM((d, d), BF16), pltpu.VMEM((d, d), BF16),
                        pltpu.VMEM((m, d), F32), pltpu.VMEM((m, d), F32),
                        pltpu.VMEM((m, d), BF16), pltpu.VMEM((m, d), BF16),
                        pltpu.VMEM((tm, d), BF16)],
        compiler_params=_params(("arbitrary",)),
        name="xattn",
    )(x, mem, wq_all, wk_all, wv_all, wo_all, g_all, b_all)


def _ffn_kernel(x_ref, wg_ref, wv_ref, wd_ref, cw_ref, cb_ref, g_ref, b_ref, o_ref,
                xb_ref, carry_ref, gext_ref, *, layer):
    i = pl.program_id(0)
    j = pl.program_id(1)
    tm = x_ref.shape[0]
    lrow = slice(layer, layer + 1)

    @pl.when(j == 0)
    def _():
        xb_ref[...] = x_ref[...].astype(BF16)
        o_ref[...] = jnp.zeros_like(o_ref)

    @pl.when(i == 0)
    def _():
        carry_ref[j] = jnp.zeros(carry_ref.shape[1:], F32)

    xb = xb_ref[...]
    gate = _dot(xb, wg_ref[...])
    val = _dot(xb, wv_ref[...])
    gext_ref[0:SUBLANES, :] = carry_ref[j]
    gext_ref[SUBLANES:SUBLANES + tm, :] = gate
    carry_ref[j] = gate[tm - SUBLANES:tm, :]
    conv = cb_ref[lrow, :] + cw_ref[FFN_CONV - 1:FFN_CONV, :] * gate
    for t in range(FFN_CONV - 1):
        off = SUBLANES - (FFN_CONV - 1) + t
        conv = conv + cw_ref[t:t + 1, :] * gext_ref[off:off + tm, :]
    act = 0.5 * conv * (1.0 + lax.erf(conv * math.sqrt(0.5)))
    o_ref[...] += _dot((act * val).astype(BF16), wd_ref[...])

    @pl.when(j == pl.num_programs(1) - 1)
    def _():
        o_ref[...] = _layer_norm(ALPHA * x_ref[...] + o_ref[...], g_ref[lrow, :], b_ref[lrow, :])


def _ffn(x, wup, wdn, cw_all, cb_all, g_all, b_all, layer, tm=512, tf=512):
    s, d = x.shape
    f = wdn.shape[0]
    nf = f // tf
    nl = cb_all.shape[0]
    return pl.pallas_call(
        functools.partial(_ffn_kernel, layer=layer),
        grid=(s // tm, nf),
        in_specs=[pl.BlockSpec((tm, d), lambda i, j: (i, 0)),
                  pl.BlockSpec((d, tf), lambda i, j: (0, j)),
                  pl.BlockSpec((d, tf), lambda i, j: (0, j + nf)),
                  pl.BlockSpec((tf, d), lambda i, j: (j, 0)),
                  pl.BlockSpec((None, FFN_CONV, tf), lambda i, j: (layer, 0, j)),
                  pl.BlockSpec((nl, tf), lambda i, j: (0, j)),
                  pl.BlockSpec((nl, d), lambda i, j: (0, 0)),
                  pl.BlockSpec((nl, d), lambda i, j: (0, 0))],
        out_specs=pl.BlockSpec((tm, d), lambda i, j: (i, 0)),
        out_shape=jax.ShapeDtypeStruct((s, d), F32),
        scratch_shapes=[pltpu.VMEM((tm, d), BF16),
                        pltpu.VMEM((nf, SUBLANES, tf), F32), pltpu.VMEM((SUBLANES + tm, tf), F32)],
        compiler_params=_params(("arbitrary", "arbitrary")),
        name="ffn",
    )(x, wup, wup, wdn, cw_all, cb_all, g_all, b_all)


def _at_lanes(table, start):
    return jnp.pad(table, ((0, 0), (start, LANES - start - table.shape[1])))


def kernel(x, mem, w_in, gla_w_gate, gla_b_gate, gla_norm_w, ssd_conv_w, ssd_conv_b, ssd_dt_bias, ssd_a_log, ssd_d, ssd_norm_w, w_out, ln_mix_g, ln_mix_b, xa_wq, xa_wk, xa_wv, xa_wo, ln_xa_g, ln_xa_b, ffn_w_up, ffn_conv_w, ffn_conv_b, ffn_w_down, ln_ffn_g, ln_ffn_b):
    assert x.shape == (1, SEQ, D_MODEL) and mem.shape == (1, N_MEM, D_MODEL)
    assert w_in.shape == (DEPTH, D_MODEL, D_IN)
    xs = x[0]
    wg_all = jnp.pad(gla_w_gate, ((0, 0), (0, LANES - GLA_RANK), (0, 0))).astype(BF16)
    mixer_tables = (gla_b_gate, gla_norm_w, ssd_conv_w, ssd_conv_b, _at_lanes(ssd_dt_bias, DT_LANE),
                    _at_lanes(ssd_a_log, DT_LANE), jnp.repeat(ssd_d, SSD_HEADDIM, axis=1), ssd_norm_w)
    w_in_t = jnp.swapaxes(w_in, 1, 2)
    for l in range(DEPTH):
        qkvog, z, xbc, sm = _inproj(xs, w_in_t, l)
        mixed, wup, wdn = _mixer(qkvog, z, xbc, sm, wg_all, mixer_tables, ffn_w_up, ffn_w_down, l)
        xs = _proj_ln(mixed, w_out, xs, ln_mix_g, ln_mix_b, l)
        xs = _xattn(xs, mem, xa_wq, xa_wk, xa_wv, xa_wo, ln_xa_g, ln_xa_b, l)
        xs = _ffn(xs, wup, wdn, ffn_conv_w, ffn_conv_b, ln_ffn_g, ln_ffn_b, l)
    return xs[None]
```

```python
import functools
import math

import jax
import jax.numpy as jnp
import numpy as np
from jax import lax
from jax.experimental import pallas as pl
from jax.experimental.pallas import tpu as pltpu

F32 = jnp.float32
BF16 = jnp.bfloat16

D_MODEL = 2048
SEQ = 8192
DEPTH = 4
CHUNK = 64
N_MEM = 256
LN_EPS = 1e-5
RMS_EPS = 1e-6
ALPHA = (2.0 * DEPTH) ** 0.25

GLA_HEADS = 4
GLA_DV = D_MODEL // 2
GLA_HEAD_V = GLA_DV // GLA_HEADS
GLA_DK = GLA_DV // 2
GLA_HEAD_K = GLA_DK // GLA_HEADS
GLA_RANK = 16
GLA_TAU = 16.0
SSD_D_INNER = D_MODEL - GLA_DV
SSD_HEADDIM = 64
SSD_HEADS = SSD_D_INNER // SSD_HEADDIM
SSD_STATE = 128
SSD_GROUPS = 2
SSD_HPG = SSD_HEADS // SSD_GROUPS
SSD_CONV = 4
SSD_BC = SSD_GROUPS * SSD_STATE
SSD_CONV_DIM = SSD_D_INNER + 2 * SSD_BC
SSD_GROUP_W = SSD_D_INNER // SSD_GROUPS
XA_HEADS = 4
XA_HEAD_DIM = D_MODEL // XA_HEADS
D_FF = ((8 * D_MODEL // 3 + 255) // 256) * 256
FFN_CONV = 3

LANES = 128
SUBLANES = 8
QKVOG_W = 2 * GLA_DK + 2 * GLA_DV
SMALL_W = 2 * LANES
D_IN = QKVOG_W + GLA_RANK + SSD_D_INNER + SSD_CONV_DIM + SSD_HEADS
W_IN_PAD = -(-D_IN // LANES) * LANES
W_IN_FLOOR = D_IN // LANES * LANES
Z_END_PAD = -(-(QKVOG_W + GLA_RANK + SSD_D_INNER) // LANES) * LANES
XBC_START_ALIGNED = (QKVOG_W + GLA_RANK + SSD_D_INNER) // LANES * LANES
DT_LANE = D_IN - SSD_HEADS - (W_IN_PAD - LANES)
LOAD_ROWS = 128
INPROJ_LOAD_ROWS = 128
ROW_GROUPS = 4
XA_ROW_GROUPS = 1
VMEM_LIMIT = 56 * 1024 * 1024

NT = (((1,), (1,)), ((), ()))
TN = (((0,), (0,)), ((), ()))


def _dot(a, b):
    return jnp.dot(a, b, preferred_element_type=F32)


def _dotg(a, b, dims):
    return lax.dot_general(a, b, dims, preferred_element_type=F32)


def _split3(v):
    hi = v.astype(BF16)
    r = v - hi.astype(F32)
    mid = r.astype(BF16)
    lo = (r - mid.astype(F32)).astype(BF16)
    return hi, mid, lo


def _exact_left(mat, v):
    hi, mid, lo = _split3(v)
    return _dot(mat, hi) + _dot(mat, mid) + _dot(mat, lo)


def _exact_right(v, mat):
    hi, mid, lo = _split3(v)
    return _dot(hi, mat) + _dot(mid, mat) + _dot(lo, mat)


def _sigmoid(x):
    return 1.0 / (1.0 + jnp.exp(-x))


def _softplus(x):
    return jnp.maximum(x, 0.0) + jnp.log1p(jnp.exp(-jnp.abs(x)))


def _layer_norm(y, g, b):
    mu = jnp.mean(y, axis=-1, keepdims=True)
    d = y - mu
    var = jnp.mean(d * d, axis=-1, keepdims=True)
    return d * lax.rsqrt(var + LN_EPS) * g + b


def _whole(arr):
    nd = arr.ndim
    return pl.BlockSpec(arr.shape, lambda *_: (0,) * nd, pipeline_mode=pl.Buffered(1))


def _params(sem):
    return pltpu.CompilerParams(dimension_semantics=sem, vmem_limit_bytes=VMEM_LIMIT)


def _weight_chunk(layer, n_load, cols):
    return pl.BlockSpec((None, LOAD_ROWS, cols), lambda s: (layer, jnp.minimum(s, n_load - 1), 0))


def _seq_tile(n_load, tm, cols):
    return pl.BlockSpec((tm, cols), lambda s: (jnp.maximum(s - n_load, 0), 0))


def _chunk_rows(step):
    return pl.ds(pl.multiple_of(step * LOAD_ROWS, LOAD_ROWS), LOAD_ROWS)


def _projection_parts(x_ref, wb_ref, slab):
    o1_ref, oz_ref, ox_ref, os_ref = slab
    xb = x_ref[...].astype(BF16)
    half = QKVOG_W // 2

    def qkvog_part(lo_col):
        def run():
            o1_ref[:, lo_col:lo_col + half] = _dotg(xb, wb_ref[lo_col:lo_col + half, :], NT)
        return run

    def gate_z_part():
        lo = _dotg(xb, wb_ref[QKVOG_W:Z_END_PAD, :], NT)
        os_ref[:, 0:LANES] = lo[:, 0:LANES]
        oz_ref[...] = lo[:, GLA_RANK:GLA_RANK + SSD_D_INNER]

    def xbc_dt_part():
        hi = _dotg(xb, wb_ref[XBC_START_ALIGNED:W_IN_PAD, :], NT)
        ox_ref[...] = hi[:, GLA_RANK:GLA_RANK + SSD_CONV_DIM]
        os_ref[:, LANES:2 * LANES] = hi[:, W_IN_PAD - XBC_START_ALIGNED - LANES:W_IN_PAD - XBC_START_ALIGNED]

    return [qkvog_part(0), qkvog_part(half), gate_z_part, xbc_dt_part]


def _inmix_kernel(xa_ref, xb_ref, wt_ref, wg_ref, bg_ref, gnw_ref, cw_ref, cb_ref, dtb_ref, a_ref, dexp_ref,
                  snw_ref, e_ref, tri_ref, trile_ref, diage_ref, bd_ref, out_ref,
                  wb_ref, qa_ref, za_ref, ca_ref, sa_ref, qb_ref, zb_ref, cb2_ref, sb_ref,
                  sgla_ref, sssd_ref, xprev_ref, *, n_load, layer):
    s = pl.program_id(0)
    lrow = slice(layer, layer + 1)
    full_rows = (n_load - 1) * INPROJ_LOAD_ROWS
    last_rows = D_IN - full_rows
    slab_a = (qa_ref, za_ref, ca_ref, sa_ref)
    slab_b = (qb_ref, zb_ref, cb2_ref, sb_ref)
    tile = xa_ref.shape[0]

    @pl.when(s < n_load - 1)
    def _():
        rows = pl.ds(pl.multiple_of(s * INPROJ_LOAD_ROWS, INPROJ_LOAD_ROWS), INPROJ_LOAD_ROWS)
        wb_ref[rows, :] = wt_ref[...].astype(BF16)

    @pl.when(s == n_load - 1)
    def _():
        wb_ref[full_rows:D_IN, :] = wt_ref[0:last_rows, :].astype(BF16)
        wb_ref[D_IN:W_IN_PAD, :] = jnp.zeros((W_IN_PAD - D_IN, wb_ref.shape[1]), BF16)

    @pl.when(s == n_load)
    def _():
        sgla_ref[...] = jnp.zeros_like(sgla_ref)
        sssd_ref[...] = jnp.zeros_like(sssd_ref)
        xprev_ref[...] = jnp.zeros_like(xprev_ref)
        for part in _projection_parts(xb_ref, wb_ref, slab_a):
            part()

    def mix_and_project(slab, out_base, x_next_ref, slab_next):
        q_ref, z_ref, c_ref, s_ref = slab
        refs = (q_ref, z_ref, c_ref, s_ref, wg_ref, bg_ref, gnw_ref, cw_ref, cb_ref, dtb_ref, a_ref,
                dexp_ref, snw_ref, e_ref, tri_ref, trile_ref, diage_ref, bd_ref, out_ref, sgla_ref, sssd_ref)
        parts = _projection_parts(x_next_ref, wb_ref, slab_next)
        n_chunks = tile // CHUNK
        assert len(parts) == n_chunks
        tail = xprev_ref[...]
        for ci in range(n_chunks):
            rows = slice(ci * CHUNK, (ci + 1) * CHUNK)
            out_rows = slice(out_base + ci * CHUNK, out_base + (ci + 1) * CHUNK)
            parts[ci]()
            _mixer_chunk(refs, rows, out_rows, tail, lrow)
            tail = c_ref[(ci + 1) * CHUNK - SUBLANES:(ci + 1) * CHUNK, :]
        xprev_ref[...] = tail

    @pl.when(s > n_load)
    def _():
        mix_and_project(slab_a, 0, xa_ref, slab_b)
        mix_and_project(slab_b, tile, xb_ref, slab_a)


def _mixer_chunk(refs, rows, out_rows, tail, lrow):
    (qkvog_ref, z_ref, xbc_ref, sm_ref, wg_ref, bg_ref, gnw_ref, cw_ref, cb_ref, dtb_ref, a_ref,
     dexp_ref, snw_ref, e_ref, tri_ref, trile_ref, diage_ref, bd_ref, out_ref, sgla_ref, sssd_ref) = refs
    tri = tri_ref[...]
    tri_mask = tri > 0

    alr = sm_ref[rows, 0:LANES].astype(BF16)
    gpre = _dot(alr, wg_ref[...]) + bg_ref[lrow, :]
    log_g = -_softplus(-gpre) / GLA_TAU
    b = _exact_left(tri, log_g)
    b_last = b[CHUNK - 1:CHUNK, :]
    q = qkvog_ref[rows, 0:GLA_DK]
    k = qkvog_ref[rows, GLA_DK:2 * GLA_DK]
    q_dec = ((q * (GLA_HEAD_K ** -0.5)) * jnp.exp(b)).astype(BF16)
    k_inv = (k * jnp.exp(-b)).astype(BF16)
    k_end = (k * jnp.exp(b_last - b)).astype(BF16)
    e_last = jnp.exp(b_last)
    for h in range(GLA_HEADS):
        ks = slice(h * GLA_HEAD_K, (h + 1) * GLA_HEAD_K)
        vs = slice(2 * GLA_DK + h * GLA_HEAD_V, 2 * GLA_DK + (h + 1) * GLA_HEAD_V)
        gs = slice(2 * GLA_DK + GLA_DV + h * GLA_HEAD_V, 2 * GLA_DK + GLA_DV + (h + 1) * GLA_HEAD_V)
        os_ = slice(h * GLA_HEAD_V, (h + 1) * GLA_HEAD_V)
        vh = qkvog_ref[rows, vs].astype(BF16)
        scores = jnp.where(tri_mask, _dotg(q_dec[:, ks], k_inv[:, ks], NT), 0.0)
        state_t = sgla_ref[h]
        o = _dot(scores.astype(BF16), vh) + _dotg(q_dec[:, ks], state_t.astype(BF16), NT)
        sgla_ref[h] = e_last[:, ks] * state_t + _dotg(vh, k_end[:, ks], TN)
        o = o * lax.rsqrt(jnp.mean(o * o, axis=-1, keepdims=True) + RMS_EPS) * gnw_ref[lrow, os_]
        og = qkvog_ref[rows, gs]
        out_ref[out_rows, os_] = (o * (og * _sigmoid(og))).astype(BF16)

    xbc = xbc_ref[rows, :]
    xext = jnp.concatenate([tail, xbc], axis=0)
    conv = cb_ref[lrow, :] + cw_ref[SSD_CONV - 1:SSD_CONV, :] * xbc
    for back in range(1, SSD_CONV):
        shifted = pltpu.roll(xext, back, axis=0)[SUBLANES:SUBLANES + CHUNK, :]
        conv = conv + cw_ref[SSD_CONV - 1 - back:SSD_CONV - back, :] * shifted
    xc = conv * _sigmoid(conv)
    xs = xc[:, 0:SSD_D_INNER]
    dt = _softplus(sm_ref[rows, LANES:2 * LANES] + dtb_ref[lrow, :])
    cs = _exact_left(tri, dt * -jnp.exp(a_ref[lrow, :]))
    expand = e_ref[...]
    dt_e = _exact_right(dt[:, 0:DT_LANE + SSD_HEADS], expand)
    cs_e = _exact_right(cs[:, 0:DT_LANE + SSD_HEADS], expand)
    cs_last_e = cs_e[CHUNK - 1:CHUNK, :]
    xdt = xs * dt_e
    cs_row = jnp.sum(cs_e * diage_ref[...], axis=0, keepdims=True)
    lmat = jnp.where(trile_ref[...] > 0, jnp.exp(cs_e - cs_row), 0.0)
    xdt_end = (xdt * jnp.exp(cs_last_e - cs_e)).astype(BF16)
    xdt_b = xdt.astype(BF16)
    decay_in = jnp.exp(cs_e)
    decay_chunk = jnp.exp(cs_last_e)
    bd = bd_ref[...] > 0
    for g in range(SSD_GROUPS):
        gl = slice(g * SSD_GROUP_W, (g + 1) * SSD_GROUP_W)
        bm = xc[:, SSD_D_INNER + g * SSD_STATE:SSD_D_INNER + (g + 1) * SSD_STATE].astype(BF16)
        cm = xc[:, SSD_D_INNER + SSD_BC + g * SSD_STATE:
                SSD_D_INNER + SSD_BC + (g + 1) * SSD_STATE].astype(BF16)
        bm_rep = jnp.concatenate([bm] * SSD_HPG, axis=0)
        w = (_dotg(cm, bm_rep, NT) * lmat[:, gl]).astype(BF16)
        state_t = sssd_ref[g]
        y_off = _dot(cm, state_t.astype(BF16)) * decay_in[:, gl]
        sssd_ref[g] = decay_chunk[:, gl] * state_t + _dotg(bm, xdt_end[:, gl], TN)
        y_diag = []
        for hp in range(SSD_HPG // 2):
            pl_ = slice(g * SSD_GROUP_W + hp * 2 * SSD_HEADDIM, g * SSD_GROUP_W + (hp + 1) * 2 * SSD_HEADDIM)
            wl = slice(hp * 2 * CHUNK, (hp + 1) * 2 * CHUNK)
            x2 = xdt_b[:, pl_]
            x_bd = jnp.where(bd, jnp.concatenate([x2, x2], axis=0), jnp.zeros((), BF16))
            y_diag.append(_dot(w[:, wl], x_bd))
        y = jnp.concatenate(y_diag, axis=1) + y_off + dexp_ref[lrow, gl] * xs[:, gl]
        zg = z_ref[rows, gl]
        y = y * (zg * _sigmoid(zg))
        y = y * lax.rsqrt(jnp.mean(y * y, axis=-1, keepdims=True) + RMS_EPS) * snw_ref[lrow, gl]
        out_ref[out_rows, GLA_DV + g * SSD_GROUP_W:GLA_DV + (g + 1) * SSD_GROUP_W] = y.astype(BF16)


def _mixer_constants():
    l = np.arange(CHUNK)
    tri = (l[None, :] <= l[:, None]).astype(np.float32)
    lane = np.arange(SSD_D_INNER)
    m_of_lane = lane % SSD_HEADDIM
    trile = (m_of_lane[None, :] <= l[:, None]).astype(np.float32)
    diage = (m_of_lane[None, :] == l[:, None]).astype(np.float32)
    expand = np.zeros((DT_LANE + SSD_HEADS, SSD_D_INNER), np.float32)
    expand[DT_LANE + lane // SSD_HEADDIM, lane] = 1.0
    r = np.arange(2 * SSD_HEADDIM)
    bd = ((r[:, None] // SSD_HEADDIM) == (r[None, :] // SSD_HEADDIM)).astype(np.float32)
    return (jnp.asarray(expand, BF16), jnp.asarray(tri, BF16), jnp.asarray(trile), jnp.asarray(diage),
            jnp.asarray(bd))


def _inmix(x, w_in_t, wg_all, per_layer, layer, chunks_per_tile=4):
    assert CHUNK == SSD_HEADDIM
    s, d = x.shape
    consts = _mixer_constants()
    tile = chunks_per_tile * CHUNK
    n_tiles = s // tile
    assert n_tiles % 2 == 0
    n_load = -(-D_IN // INPROJ_LOAD_ROWS)
    assert (D_IN - (n_load - 1) * INPROJ_LOAD_ROWS) % (2 * SUBLANES) == 0
    first = n_load + 1
    x_tile = lambda off: pl.BlockSpec(
        (tile, d), lambda t: (jnp.clip(2 * (t - first) + off, 0, n_tiles - 1), 0))
    wt_spec = pl.BlockSpec((None, INPROJ_LOAD_ROWS, d), lambda t: (layer, jnp.minimum(t, n_load - 1), 0))
    bg, gnw, cw, cb, dtb, a_log, dexp, snw = per_layer
    layer_slab = lambda t: pl.BlockSpec((None,) + t.shape[1:], lambda i: (layer, 0, 0))
    order = (wg_all, bg, gnw, cw, cb, dtb, a_log, dexp, snw) + consts
    in_specs = [x_tile(1), x_tile(2), wt_spec, layer_slab(wg_all), _whole(bg), _whole(gnw), layer_slab(cw)]
    in_specs += [_whole(t) for t in order[4:]]
    slab = [pltpu.VMEM((tile, n), F32) for n in (QKVOG_W, SSD_D_INNER, SSD_CONV_DIM, SMALL_W)]
    return pl.pallas_call(
        functools.partial(_inmix_kernel, n_load=n_load, layer=layer),
        grid=(first + n_tiles // 2,),
        in_specs=in_specs,
        out_specs=pl.BlockSpec((2 * tile, D_MODEL), lambda t: (jnp.maximum(t - first, 0), 0)),
        out_shape=jax.ShapeDtypeStruct((s, D_MODEL), BF16),
        scratch_shapes=[pltpu.VMEM((W_IN_PAD, d), BF16)] + slab + slab + [
            pltpu.VMEM((GLA_HEADS, GLA_HEAD_V, GLA_HEAD_K), F32),
            pltpu.VMEM((SSD_GROUPS, SSD_STATE, SSD_GROUP_W), F32),
            pltpu.VMEM((SUBLANES, SSD_CONV_DIM), F32),
        ],
        compiler_params=_params(("arbitrary",)),
        name="inmix",
    )(x, x, w_in_t, *order)


def _proj_ln_kernel(a_ref, w_ref, x_ref, g_ref, b_ref, o_ref, wb_ref, *, n_load, layer):
    s = pl.program_id(0)
    lrow = slice(layer, layer + 1)

    @pl.when(s < n_load)
    def _():
        wb_ref[_chunk_rows(s), :] = w_ref[...].astype(BF16)

    @pl.when(s >= n_load)
    def _():
        half = a_ref.shape[0] // ROW_GROUPS
        for r in range(ROW_GROUPS):
            rows = slice(r * half, (r + 1) * half)
            h = _dot(a_ref[rows, :], wb_ref[...])
            o_ref[rows, :] = _layer_norm(ALPHA * x_ref[rows, :] + h, g_ref[lrow, :], b_ref[lrow, :])


def _proj_ln(a, w_all, x, g_all, b_all, layer, tm=512):
    s, d = x.shape
    kdim = a.shape[1]
    n_load = kdim // LOAD_ROWS
    return pl.pallas_call(
        functools.partial(_proj_ln_kernel, n_load=n_load, layer=layer),
        grid=(n_load + s // tm,),
        in_specs=[_seq_tile(n_load, tm, kdim), _weight_chunk(layer, n_load, d), _seq_tile(n_load, tm, d),
                  _whole(g_all), _whole(b_all)],
        out_specs=_seq_tile(n_load, tm, d),
        out_shape=jax.ShapeDtypeStruct((s, d), F32),
        scratch_shapes=[pltpu.VMEM((kdim, d), BF16)],
        compiler_params=_params(("arbitrary",)),
        name="proj_ln",
    )(a, w_all, x, g_all, b_all)


def _xattn_kernel(x_ref, mem_ref, wq_ref, wk_ref, wv_ref, wo_ref, g_ref, b_ref, o_ref,
                  wqb_ref, wob_ref, kacc_ref, vacc_ref, kb_ref, vb_ref, att_ref, *, n_load, layer):
    s = pl.program_id(0)
    lrow = slice(layer, layer + 1)

    @pl.when(s == 0)
    def _():
        kacc_ref[...] = jnp.zeros_like(kacc_ref)
        vacc_ref[...] = jnp.zeros_like(vacc_ref)

    @pl.when(s < n_load)
    def _():
        rows = _chunk_rows(s)
        wqb_ref[rows, :] = wq_ref[...].astype(BF16)
        wob_ref[rows, :] = wo_ref[...].astype(BF16)
        mb = mem_ref[...].astype(BF16)
        kacc_ref[...] += _dot(mb, wk_ref[...].astype(BF16))
        vacc_ref[...] += _dot(mb, wv_ref[...].astype(BF16))

    @pl.when(s == n_load - 1)
    def _():
        kb_ref[...] = kacc_ref[...].astype(BF16)
        vb_ref[...] = vacc_ref[...].astype(BF16)

    @pl.when(s >= n_load)
    def _():
        group = x_ref.shape[0] // XA_ROW_GROUPS
        for r in range(XA_ROW_GROUPS):
            rows = slice(r * group, (r + 1) * group)
            x = x_ref[rows, :]
            q = _dot(x.astype(BF16), wqb_ref[...])
            for h in range(XA_HEADS):
                hs = slice(h * XA_HEAD_DIM, (h + 1) * XA_HEAD_DIM)
                sc = _dotg(q[:, hs].astype(BF16), kb_ref[:, hs], NT) * (XA_HEAD_DIM ** -0.5)
                e = jnp.exp(sc - jnp.max(sc, axis=-1, keepdims=True))
                p = e / jnp.sum(e, axis=-1, keepdims=True)
                att_ref[rows, hs] = _dot(p.astype(BF16), vb_ref[:, hs]).astype(BF16)
            h_out = _dot(att_ref[rows, :], wob_ref[...])
            o_ref[rows, :] = _layer_norm(ALPHA * x + h_out, g_ref[lrow, :], b_ref[lrow, :])


def _xattn(x, mem, wq_all, wk_all, wv_all, wo_all, g_all, b_all, layer, tm=512):
    s, d = x.shape
    m = mem.shape[1]
    n_load = d // LOAD_ROWS
    mem_spec = pl.BlockSpec((None, m, LOAD_ROWS), lambda t: (0, 0, jnp.minimum(t, n_load - 1)))
    wchunk = _weight_chunk(layer, n_load, d)
    return pl.pallas_call(
        functools.partial(_xattn_kernel, n_load=n_load, layer=layer),
        grid=(n_load + s // tm,),
        in_specs=[_seq_tile(n_load, tm, d), mem_spec, wchunk, wchunk, wchunk, wchunk,
                  _whole(g_all), _whole(b_all)],
        out_specs=_seq_tile(n_load, tm, d),
        out_shape=jax.ShapeDtypeStruct((s, d), F32),
        scratch_shapes=[pltpu.VMEM((d, d), BF16), pltpu.VMEM((d, d), BF16),
                        pltpu.VMEM((m, d), F32), pltpu.VMEM((m, d), F32),
                        pltpu.VMEM((m, d), BF16), pltpu.VMEM((m, d), BF16),
                        pltpu.VMEM((tm, d), BF16)],
        compiler_params=_params(("arbitrary",)),
        name="xattn",
    )(x, mem, wq_all, wk_all, wv_all, wo_all, g_all, b_all)


def _ffn_kernel(x_ref, wg_ref, wv_ref, wd_ref, cw_ref, cb_ref, g_ref, b_ref, *rest, layer, narrow_next):
    if narrow_next:
        wup_next_ref, wdn_next_ref, o_ref, wup_out_ref, wdn_out_ref, xb_ref, carry_ref, gext_ref = rest
        wup_out_ref[...] = wup_next_ref[...].astype(BF16)
        wdn_out_ref[...] = wdn_next_ref[...].astype(BF16)
    else:
        o_ref, xb_ref, carry_ref, gext_ref = rest
    i = pl.program_id(0)
    j = pl.program_id(1)
    tm = x_ref.shape[0]
    lrow = slice(layer, layer + 1)

    @pl.when(j == 0)
    def _():
        xb_ref[...] = x_ref[...].astype(BF16)
        o_ref[...] = jnp.zeros_like(o_ref)

    @pl.when(i == 0)
    def _():
        carry_ref[j] = jnp.zeros(carry_ref.shape[1:], F32)

    xb = xb_ref[...]
    gate = _dot(xb, wg_ref[...])
    val = _dot(xb, wv_ref[...])
    gext_ref[0:SUBLANES, :] = carry_ref[j]
    gext_ref[SUBLANES:SUBLANES + tm, :] = gate
    carry_ref[j] = gate[tm - SUBLANES:tm, :]
    conv = cb_ref[lrow, :] + cw_ref[FFN_CONV - 1:FFN_CONV, :] * gate
    for t in range(FFN_CONV - 1):
        off = SUBLANES - (FFN_CONV - 1) + t
        conv = conv + cw_ref[t:t + 1, :] * gext_ref[off:off + tm, :]
    act = 0.5 * conv * (1.0 + lax.erf(conv * math.sqrt(0.5)))
    o_ref[...] += _dot((act * val).astype(BF16), wd_ref[...])

    @pl.when(j == pl.num_programs(1) - 1)
    def _():
        o_ref[...] = _layer_norm(ALPHA * x_ref[...] + o_ref[...], g_ref[lrow, :], b_ref[lrow, :])


def _ffn(x, wup, wdn, cw_all, cb_all, g_all, b_all, layer, w_up_all=None, w_down_all=None, tm=512, tf=512):
    s, d = x.shape
    f = wdn.shape[0]
    nf = f // tf
    ni = s // tm
    nl = cb_all.shape[0]
    narrow_next = w_up_all is not None
    in_specs = [pl.BlockSpec((tm, d), lambda i, j: (i, 0)),
                pl.BlockSpec((d, tf), lambda i, j: (0, j)),
                pl.BlockSpec((d, tf), lambda i, j: (0, j + nf)),
                pl.BlockSpec((tf, d), lambda i, j: (j, 0)),
                pl.BlockSpec((None, FFN_CONV, tf), lambda i, j: (layer, 0, j)),
                pl.BlockSpec((nl, tf), lambda i, j: (0, j)),
                pl.BlockSpec((nl, d), lambda i, j: (0, 0)),
                pl.BlockSpec((nl, d), lambda i, j: (0, 0))]
    out_specs = [pl.BlockSpec((tm, d), lambda i, j: (i, 0))]
    out_shape = [jax.ShapeDtypeStruct((s, d), F32)]
    operands = [x, wup, wup, wdn, cw_all, cb_all, g_all, b_all]
    if narrow_next:
        up_blk = (d // ni, 2 * f // nf)
        dn_blk = (f // nf, d // ni)
        assert up_blk[0] * ni == d and up_blk[1] * nf == 2 * f and up_blk[0] % (2 * SUBLANES) == 0
        assert dn_blk[0] * nf == f and dn_blk[1] * ni == d and up_blk[1] % LANES == 0 and dn_blk[1] % LANES == 0
        in_specs += [pl.BlockSpec((None,) + up_blk, lambda i, j: (layer + 1, i, j)),
                     pl.BlockSpec((None,) + dn_blk, lambda i, j: (layer + 1, j, i))]
        out_specs += [pl.BlockSpec(up_blk, lambda i, j: (i, j)), pl.BlockSpec(dn_blk, lambda i, j: (j, i))]
        out_shape += [jax.ShapeDtypeStruct((d, 2 * f), BF16), jax.ShapeDtypeStruct((f, d), BF16)]
        operands += [w_up_all, w_down_all]
    return pl.pallas_call(
        functools.partial(_ffn_kernel, layer=layer, narrow_next=narrow_next),
        grid=(ni, nf),
        in_specs=in_specs,
        out_specs=out_specs,
        out_shape=out_shape,
        scratch_shapes=[pltpu.VMEM((tm, d), BF16),
                        pltpu.VMEM((nf, SUBLANES, tf), F32), pltpu.VMEM((SUBLANES + tm, tf), F32)],
        compiler_params=_params(("arbitrary", "arbitrary")),
        name="ffn",
    )(*operands)


def _at_lanes(table, start):
    return jnp.pad(table, ((0, 0), (start, LANES - start - table.shape[1])))


def kernel(x, mem, w_in, gla_w_gate, gla_b_gate, gla_norm_w, ssd_conv_w, ssd_conv_b, ssd_dt_bias, ssd_a_log, ssd_d, ssd_norm_w, w_out, ln_mix_g, ln_mix_b, xa_wq, xa_wk, xa_wv, xa_wo, ln_xa_g, ln_xa_b, ffn_w_up, ffn_conv_w, ffn_conv_b, ffn_w_down, ln_ffn_g, ln_ffn_b):
    assert x.shape == (1, SEQ, D_MODEL) and mem.shape == (1, N_MEM, D_MODEL)
    assert w_in.shape == (DEPTH, D_MODEL, D_IN)
    xs = x[0]
    wg_all = jnp.pad(gla_w_gate, ((0, 0), (0, LANES - GLA_RANK), (0, 0))).astype(BF16)
    mixer_tables = (gla_b_gate, gla_norm_w, ssd_conv_w, ssd_conv_b, _at_lanes(ssd_dt_bias, DT_LANE),
                    _at_lanes(ssd_a_log, DT_LANE), jnp.repeat(ssd_d, SSD_HEADDIM, axis=1), ssd_norm_w)
    w_in_t = jnp.swapaxes(w_in, 1, 2)
    wup = ffn_w_up[0].astype(BF16)
    wdn = ffn_w_down[0].astype(BF16)
    for l in range(DEPTH):
        mixed = _inmix(xs, w_in_t, wg_all, mixer_tables, l)
        xs = _proj_ln(mixed, w_out, xs, ln_mix_g, ln_mix_b, l)
        xs = _xattn(xs, mem, xa_wq, xa_wk, xa_wv, xa_wo, ln_xa_g, ln_xa_b, l)
        if l + 1 < DEPTH:
            xs, wup, wdn = _ffn(xs, wup, wdn, ffn_conv_w, ffn_conv_b, ln_ffn_g, ln_ffn_b, l,
                                ffn_w_up, ffn_w_down)
        else:
            (xs,) = _ffn(xs, wup, wdn, ffn_conv_w, ffn_conv_b, ln_ffn_g, ln_ffn_b, l)
    return xs[None]
```

```python
import functools
import math

import jax
import jax.numpy as jnp
import numpy as np
from jax import lax
from jax.experimental import pallas as pl
from jax.experimental.pallas import tpu as pltpu

F32 = jnp.float32
BF16 = jnp.bfloat16

D_MODEL = 2048
SEQ = 8192
DEPTH = 4
CHUNK = 64
N_MEM = 256
LN_EPS = 1e-5
RMS_EPS = 1e-6
ALPHA = (2.0 * DEPTH) ** 0.25

GLA_HEADS = 4
GLA_DV = D_MODEL // 2
GLA_HEAD_V = GLA_DV // GLA_HEADS
GLA_DK = GLA_DV // 2
GLA_HEAD_K = GLA_DK // GLA_HEADS
GLA_RANK = 16
GLA_TAU = 16.0
SSD_D_INNER = D_MODEL - GLA_DV
SSD_HEADDIM = 64
SSD_HEADS = SSD_D_INNER // SSD_HEADDIM
SSD_STATE = 128
SSD_GROUPS = 2
SSD_HPG = SSD_HEADS // SSD_GROUPS
SSD_CONV = 4
SSD_BC = SSD_GROUPS * SSD_STATE
SSD_CONV_DIM = SSD_D_INNER + 2 * SSD_BC
SSD_GROUP_W = SSD_D_INNER // SSD_GROUPS
XA_HEADS = 4
XA_HEAD_DIM = D_MODEL // XA_HEADS
D_FF = ((8 * D_MODEL // 3 + 255) // 256) * 256
FFN_CONV = 3

LANES = 128
SUBLANES = 8
QKVOG_W = 2 * GLA_DK + 2 * GLA_DV
D_IN = QKVOG_W + GLA_RANK + SSD_D_INNER + SSD_CONV_DIM + SSD_HEADS
W_IN_PAD = -(-D_IN // LANES) * LANES
W_IN_FLOOR = D_IN // LANES * LANES
Z_END_PAD = -(-(QKVOG_W + GLA_RANK + SSD_D_INNER) // LANES) * LANES
XBC_START_ALIGNED = (QKVOG_W + GLA_RANK + SSD_D_INNER) // LANES * LANES
DT_LANE = D_IN - SSD_HEADS - (W_IN_PAD - LANES)
LOAD_ROWS = 128
INPROJ_LOAD_ROWS = 256
ROW_GROUPS = 4
XA_ROW_GROUPS = 1
VMEM_LIMIT = 56 * 1024 * 1024

NT = (((1,), (1,)), ((), ()))
TN = (((0,), (0,)), ((), ()))


def _dot(a, b):
    return jnp.dot(a, b, preferred_element_type=F32)


def _dotg(a, b, dims):
    return lax.dot_general(a, b, dims, preferred_element_type=F32)


def _split3(v):
    hi = v.astype(BF16)
    r = v - hi.astype(F32)
    mid = r.astype(BF16)
    lo = (r - mid.astype(F32)).astype(BF16)
    return hi, mid, lo


def _exact_left(mat, v):
    hi, mid, lo = _split3(v)
    return _dot(mat, hi) + _dot(mat, mid) + _dot(mat, lo)


def _exact_right(v, mat):
    hi, mid, lo = _split3(v)
    return _dot(hi, mat) + _dot(mid, mat) + _dot(lo, mat)


def _sigmoid(x):
    return 1.0 / (1.0 + jnp.exp(-x))


def _softplus(x):
    return jnp.maximum(x, 0.0) + jnp.log1p(jnp.exp(-jnp.abs(x)))


def _layer_norm(y, g, b):
    mu = jnp.mean(y, axis=-1, keepdims=True)
    d = y - mu
    var = jnp.mean(d * d, axis=-1, keepdims=True)
    return d * lax.rsqrt(var + LN_EPS) * g + b


def _whole(arr):
    nd = arr.ndim
    return pl.BlockSpec(arr.shape, lambda *_: (0,) * nd, pipeline_mode=pl.Buffered(1))


def _params(sem):
    return pltpu.CompilerParams(dimension_semantics=sem, vmem_limit_bytes=VMEM_LIMIT)


def _weight_chunk(layer, n_load, cols):
    return pl.BlockSpec((None, LOAD_ROWS, cols), lambda s: (layer, jnp.minimum(s, n_load - 1), 0))


def _seq_tile(n_load, tm, cols):
    return pl.BlockSpec((tm, cols), lambda s: (jnp.maximum(s - n_load, 0), 0))


def _chunk_rows(step):
    return pl.ds(pl.multiple_of(step * LOAD_ROWS, LOAD_ROWS), LOAD_ROWS)


def _inproj_kernel(x_ref, wt_ref, wg_ref, bg_ref, cw_ref, cb_ref, dtb_ref,
                   o1_ref, oz_ref, ox_ref, odt_ref, olg_ref, wb_ref, carry_ref, *, n_load, layer):
    s = pl.program_id(0)
    lrow = slice(layer, layer + 1)
    tm = x_ref.shape[0]
    full_rows = (n_load - 1) * INPROJ_LOAD_ROWS
    last_rows = D_IN - full_rows

    @pl.when(s < n_load - 1)
    def _():
        rows = pl.ds(pl.multiple_of(s * INPROJ_LOAD_ROWS, INPROJ_LOAD_ROWS), INPROJ_LOAD_ROWS)
        wb_ref[rows, :] = wt_ref[...].astype(BF16)

    @pl.when(s == n_load - 1)
    def _():
        wb_ref[full_rows:D_IN, :] = wt_ref[0:last_rows, :].astype(BF16)
        wb_ref[D_IN:W_IN_PAD, :] = jnp.zeros((W_IN_PAD - D_IN, wb_ref.shape[1]), BF16)
        carry_ref[...] = jnp.zeros_like(carry_ref)

    @pl.when(s >= n_load)
    def _():
        xb = x_ref[...].astype(BF16)
        hi = _dotg(xb, wb_ref[XBC_START_ALIGNED:W_IN_PAD, :], NT)
        xbc = hi[:, GLA_RANK:GLA_RANK + SSD_CONV_DIM]
        xext = jnp.concatenate([carry_ref[...], xbc], axis=0)
        carry_ref[...] = xbc[tm - SUBLANES:tm, :]
        conv = cb_ref[lrow, :] + cw_ref[SSD_CONV - 1:SSD_CONV, :] * xbc
        for back in range(1, SSD_CONV):
            shifted = pltpu.roll(xext, back, axis=0)[SUBLANES:SUBLANES + tm, :]
            conv = conv + cw_ref[SSD_CONV - 1 - back:SSD_CONV - back, :] * shifted
        ox_ref[...] = conv * _sigmoid(conv)
        dt_raw = hi[:, W_IN_PAD - XBC_START_ALIGNED - LANES:W_IN_PAD - XBC_START_ALIGNED]
        odt_ref[...] = _softplus(dt_raw + dtb_ref[lrow, :])
        lo = _dotg(xb, wb_ref[QKVOG_W:Z_END_PAD, :], NT)
        alr = lo[:, 0:LANES].astype(BF16)
        olg_ref[...] = -_softplus(-(_dot(alr, wg_ref[...]) + bg_ref[lrow, :])) / GLA_TAU
        z = lo[:, GLA_RANK:GLA_RANK + SSD_D_INNER]
        oz_ref[...] = z * _sigmoid(z)
        qkvog = _dotg(xb, wb_ref[0:QKVOG_W, :], NT)
        og = qkvog[:, 2 * GLA_DK + GLA_DV:QKVOG_W]
        o1_ref[:, 2 * GLA_DK + GLA_DV:QKVOG_W] = og * _sigmoid(og)
        o1_ref[:, 0:2 * GLA_DK + GLA_DV] = qkvog[:, 0:2 * GLA_DK + GLA_DV]


def _inproj(x, w_in_t, wg_all, bg, cw, cb, dtb, layer, tm=256):
    s, d = x.shape
    n_load = -(-D_IN // INPROJ_LOAD_ROWS)
    assert (D_IN - (n_load - 1) * INPROJ_LOAD_ROWS) % (2 * SUBLANES) == 0
    widths = (QKVOG_W, SSD_D_INNER, SSD_CONV_DIM, LANES, GLA_DK)
    wt_spec = pl.BlockSpec((None, INPROJ_LOAD_ROWS, d), lambda t: (layer, jnp.minimum(t, n_load - 1), 0))
    layer_slab = lambda t: pl.BlockSpec((None,) + t.shape[1:], lambda i: (layer, 0, 0))
    return pl.pallas_call(
        functools.partial(_inproj_kernel, n_load=n_load, layer=layer),
        grid=(n_load + s // tm,),
        in_specs=[_seq_tile(n_load, tm, d), wt_spec, layer_slab(wg_all), _whole(bg), layer_slab(cw),
                  _whole(cb), _whole(dtb)],
        out_specs=[_seq_tile(n_load, tm, n) for n in widths],
        out_shape=[jax.ShapeDtypeStruct((s, n), F32) for n in widths],
        scratch_shapes=[pltpu.VMEM((W_IN_PAD, d), BF16), pltpu.VMEM((SUBLANES, SSD_CONV_DIM), F32)],
        compiler_params=_params(("arbitrary",)),
        name="inproj",
    )(x, w_in_t, wg_all, bg, cw, cb, dtb)


def _mixer_kernel(qkvog_ref, zg_ref, xc_ref, dt_ref, lg_ref, gnw_ref, a_ref, dexp_ref, snw_ref,
                  e_ref, tri_ref, trile_ref, diage_ref, bd_ref,
                  wup_ref, wdn_ref, out_ref, wup_b_ref, wdn_b_ref, sgla_ref, sssd_ref, *, layer):
    c = pl.program_id(0)
    lrow = slice(layer, layer + 1)
    wup_b_ref[...] = wup_ref[...].astype(BF16)
    wdn_b_ref[...] = wdn_ref[...].astype(BF16)

    @pl.when(c == 0)
    def _():
        sgla_ref[...] = jnp.zeros_like(sgla_ref)
        sssd_ref[...] = jnp.zeros_like(sssd_ref)

    refs = (qkvog_ref, zg_ref, xc_ref, dt_ref, lg_ref, gnw_ref, a_ref, dexp_ref, snw_ref,
            e_ref, tri_ref, trile_ref, diage_ref, bd_ref, out_ref, sgla_ref, sssd_ref)
    for ci in range(qkvog_ref.shape[0] // CHUNK):
        _mixer_chunk(refs, slice(ci * CHUNK, (ci + 1) * CHUNK), lrow)


def _mixer_chunk(refs, rows, lrow):
    (qkvog_ref, zg_ref, xc_ref, dt_ref, lg_ref, gnw_ref, a_ref, dexp_ref, snw_ref,
     e_ref, tri_ref, trile_ref, diage_ref, bd_ref, out_ref, sgla_ref, sssd_ref) = refs
    tri = tri_ref[...]
    tri_mask = tri > 0

    b = _exact_left(tri, lg_ref[rows, :])
    b_last = b[CHUNK - 1:CHUNK, :]
    q = qkvog_ref[rows, 0:GLA_DK]
    k = qkvog_ref[rows, GLA_DK:2 * GLA_DK]
    q_dec = ((q * (GLA_HEAD_K ** -0.5)) * jnp.exp(b)).astype(BF16)
    k_inv = (k * jnp.exp(-b)).astype(BF16)
    k_end = (k * jnp.exp(b_last - b)).astype(BF16)
    e_last = jnp.exp(b_last)
    for h in range(GLA_HEADS):
        ks = slice(h * GLA_HEAD_K, (h + 1) * GLA_HEAD_K)
        vs = slice(2 * GLA_DK + h * GLA_HEAD_V, 2 * GLA_DK + (h + 1) * GLA_HEAD_V)
        gs = slice(2 * GLA_DK + GLA_DV + h * GLA_HEAD_V, 2 * GLA_DK + GLA_DV + (h + 1) * GLA_HEAD_V)
        os_ = slice(h * GLA_HEAD_V, (h + 1) * GLA_HEAD_V)
        vh = qkvog_ref[rows, vs].astype(BF16)
        scores = jnp.where(tri_mask, _dotg(q_dec[:, ks], k_inv[:, ks], NT), 0.0)
        state_t = sgla_ref[h]
        o = _dot(scores.astype(BF16), vh) + _dotg(q_dec[:, ks], state_t.astype(BF16), NT)
        sgla_ref[h] = e_last[:, ks] * state_t + _dotg(vh, k_end[:, ks], TN)
        o = o * lax.rsqrt(jnp.mean(o * o, axis=-1, keepdims=True) + RMS_EPS) * gnw_ref[lrow, os_]
        out_ref[rows, os_] = (o * qkvog_ref[rows, gs]).astype(BF16)

    xs = xc_ref[rows, 0:SSD_D_INNER]
    dt = dt_ref[rows, :]
    cs = _exact_left(tri, dt * -jnp.exp(a_ref[lrow, :]))
    expand = e_ref[...]
    dt_e = _exact_right(dt[:, 0:DT_LANE + SSD_HEADS], expand)
    cs_e = _exact_right(cs[:, 0:DT_LANE + SSD_HEADS], expand)
    cs_last_e = cs_e[CHUNK - 1:CHUNK, :]
    xdt = xs * dt_e
    cs_row = jnp.sum(cs_e * diage_ref[...], axis=0, keepdims=True)
    lmat = jnp.where(trile_ref[...] > 0, jnp.exp(cs_e - cs_row), 0.0)
    xdt_end = (xdt * jnp.exp(cs_last_e - cs_e)).astype(BF16)
    xdt_b = xdt.astype(BF16)
    decay_in = jnp.exp(cs_e)
    decay_chunk = jnp.exp(cs_last_e)
    bd = bd_ref[...] > 0
    for g in range(SSD_GROUPS):
        gl = slice(g * SSD_GROUP_W, (g + 1) * SSD_GROUP_W)
        bm = xc_ref[rows, SSD_D_INNER + g * SSD_STATE:SSD_D_INNER + (g + 1) * SSD_STATE].astype(BF16)
        cm = xc_ref[rows, SSD_D_INNER + SSD_BC + g * SSD_STATE:
                    SSD_D_INNER + SSD_BC + (g + 1) * SSD_STATE].astype(BF16)
        bm_rep = jnp.concatenate([bm] * SSD_HPG, axis=0)
        w = (_dotg(cm, bm_rep, NT) * lmat[:, gl]).astype(BF16)
        state_t = sssd_ref[g]
        y_off = _dot(cm, state_t.astype(BF16)) * decay_in[:, gl]
        sssd_ref[g] = decay_chunk[:, gl] * state_t + _dotg(bm, xdt_end[:, gl], TN)
        y_diag = []
        for hp in range(SSD_HPG // 2):
            pl_ = slice(g * SSD_GROUP_W + hp * 2 * SSD_HEADDIM, g * SSD_GROUP_W + (hp + 1) * 2 * SSD_HEADDIM)
            wl = slice(hp * 2 * CHUNK, (hp + 1) * 2 * CHUNK)
            x2 = xdt_b[:, pl_]
            x_bd = jnp.where(bd, jnp.concatenate([x2, x2], axis=0), jnp.zeros((), BF16))
            y_diag.append(_dot(w[:, wl], x_bd))
        y = jnp.concatenate(y_diag, axis=1) + y_off + dexp_ref[lrow, gl] * xs[:, gl]
        y = y * zg_ref[rows, gl]
        y = y * lax.rsqrt(jnp.mean(y * y, axis=-1, keepdims=True) + RMS_EPS) * snw_ref[lrow, gl]
        out_ref[rows, GLA_DV + g * SSD_GROUP_W:GLA_DV + (g + 1) * SSD_GROUP_W] = y.astype(BF16)


def _mixer_constants():
    l = np.arange(CHUNK)
    tri = (l[None, :] <= l[:, None]).astype(np.float32)
    lane = np.arange(SSD_D_INNER)
    m_of_lane = lane % SSD_HEADDIM
    trile = (m_of_lane[None, :] <= l[:, None]).astype(np.float32)
    diage = (m_of_lane[None, :] == l[:, None]).astype(np.float32)
    expand = np.zeros((DT_LANE + SSD_HEADS, SSD_D_INNER), np.float32)
    expand[DT_LANE + lane // SSD_HEADDIM, lane] = 1.0
    r = np.arange(2 * SSD_HEADDIM)
    bd = ((r[:, None] // SSD_HEADDIM) == (r[None, :] // SSD_HEADDIM)).astype(np.float32)
    return (jnp.asarray(expand, BF16), jnp.asarray(tri, BF16), jnp.asarray(trile), jnp.asarray(diage),
            jnp.asarray(bd))


def _mixer(qkvog, zg, xc, dt, lg, per_layer, ffn_w_up, ffn_w_down, layer, chunks_per_step=4):
    assert CHUNK == SSD_HEADDIM
    s = qkvog.shape[0]
    consts = _mixer_constants()
    tm = chunks_per_step * CHUNK
    steps = s // tm
    row = lambda n: pl.BlockSpec((tm, n), lambda i: (i, 0))
    order = tuple(per_layer) + consts
    in_specs = [row(QKVOG_W), row(SSD_D_INNER), row(SSD_CONV_DIM), row(LANES), row(GLA_DK)]
    in_specs += [_whole(t) for t in order]
    up_rows, up_cols = ffn_w_up.shape[1] // steps, ffn_w_up.shape[2]
    dn_rows, dn_cols = ffn_w_down.shape[1] // steps, ffn_w_down.shape[2]
    assert up_rows * steps == ffn_w_up.shape[1] and up_rows % (2 * SUBLANES) == 0
    assert dn_rows * steps == ffn_w_down.shape[1] and dn_rows % (2 * SUBLANES) == 0
    in_specs += [pl.BlockSpec((None, up_rows, up_cols), lambda i: (layer, i, 0)),
                 pl.BlockSpec((None, dn_rows, dn_cols), lambda i: (layer, i, 0))]
    return pl.pallas_call(
        functools.partial(_mixer_kernel, layer=layer),
        grid=(steps,),
        in_specs=in_specs,
        out_specs=[row(D_MODEL), pl.BlockSpec((up_rows, up_cols), lambda i: (i, 0)),
                   pl.BlockSpec((dn_rows, dn_cols), lambda i: (i, 0))],
        out_shape=[jax.ShapeDtypeStruct((s, D_MODEL), BF16),
                   jax.ShapeDtypeStruct(ffn_w_up.shape[1:], BF16),
                   jax.ShapeDtypeStruct(ffn_w_down.shape[1:], BF16)],
        scratch_shapes=[
            pltpu.VMEM((GLA_HEADS, GLA_HEAD_V, GLA_HEAD_K), F32),
            pltpu.VMEM((SSD_GROUPS, SSD_STATE, SSD_GROUP_W), F32),
        ],
        compiler_params=_params(("arbitrary",)),
        name="mixer",
    )(qkvog, zg, xc, dt, lg, *order, ffn_w_up, ffn_w_down)


def _proj_ln_kernel(a_ref, w_ref, x_ref, g_ref, b_ref, o_ref, wb_ref, *, n_load, layer):
    s = pl.program_id(0)
    lrow = slice(layer, layer + 1)

    @pl.when(s < n_load)
    def _():
        wb_ref[_chunk_rows(s), :] = w_ref[...].astype(BF16)

    @pl.when(s >= n_load)
    def _():
        half = a_ref.shape[0] // ROW_GROUPS
        for r in range(ROW_GROUPS):
            rows = slice(r * half, (r + 1) * half)
            h = _dot(a_ref[rows, :], wb_ref[...])
            o_ref[rows, :] = _layer_norm(ALPHA * x_ref[rows, :] + h, g_ref[lrow, :], b_ref[lrow, :])


def _proj_ln(a, w_all, x, g_all, b_all, layer, tm=512):
    s, d = x.shape
    kdim = a.shape[1]
    n_load = kdim // LOAD_ROWS
    return pl.pallas_call(
        functools.partial(_proj_ln_kernel, n_load=n_load, layer=layer),
        grid=(n_load + s // tm,),
        in_specs=[_seq_tile(n_load, tm, kdim), _weight_chunk(layer, n_load, d), _seq_tile(n_load, tm, d),
                  _whole(g_all), _whole(b_all)],
        out_specs=_seq_tile(n_load, tm, d),
        out_shape=jax.ShapeDtypeStruct((s, d), F32),
        scratch_shapes=[pltpu.VMEM((kdim, d), BF16)],
        compiler_params=_params(("arbitrary",)),
        name="proj_ln",
    )(a, w_all, x, g_all, b_all)


def _xattn_kernel(x_ref, mem_ref, wq_ref, wk_ref, wv_ref, wo_ref, g_ref, b_ref, o_ref,
                  wqb_ref, wob_ref, kacc_ref, vacc_ref, kb_ref, vb_ref, att_ref, *, n_load, layer):
    s = pl.program_id(0)
    lrow = slice(layer, layer + 1)

    @pl.when(s == 0)
    def _():
        kacc_ref[...] = jnp.zeros_like(kacc_ref)
        vacc_ref[...] = jnp.zeros_like(vacc_ref)

    @pl.when(s < n_load)
    def _():
        rows = _chunk_rows(s)
        wqb_ref[rows, :] = wq_ref[...].astype(BF16)
        wob_ref[rows, :] = wo_ref[...].astype(BF16)
        mb = mem_ref[...].astype(BF16)
        kacc_ref[...] += _dot(mb, wk_ref[...].astype(BF16))
        vacc_ref[...] += _dot(mb, wv_ref[...].astype(BF16))

    @pl.when(s == n_load - 1)
    def _():
        kb_ref[...] = kacc_ref[...].astype(BF16)
        vb_ref[...] = vacc_ref[...].astype(BF16)

    @pl.when(s >= n_load)
    def _():
        group = x_ref.shape[0] // XA_ROW_GROUPS
        for r in range(XA_ROW_GROUPS):
            rows = slice(r * group, (r + 1) * group)
            x = x_ref[rows, :]
            q = _dot(x.astype(BF16), wqb_ref[...])
            for h in range(XA_HEADS):
                hs = slice(h * XA_HEAD_DIM, (h + 1) * XA_HEAD_DIM)
                sc = _dotg(q[:, hs].astype(BF16), kb_ref[:, hs], NT) * (XA_HEAD_DIM ** -0.5)
                e = jnp.exp(sc - jnp.max(sc, axis=-1, keepdims=True))
                p = e / jnp.sum(e, axis=-1, keepdims=True)
                att_ref[rows, hs] = _dot(p.astype(BF16), vb_ref[:, hs]).astype(BF16)
            h_out = _dot(att_ref[rows, :], wob_ref[...])
            o_ref[rows, :] = _layer_norm(ALPHA * x + h_out, g_ref[lrow, :], b_ref[lrow, :])


def _xattn(x, mem, wq_all, wk_all, wv_all, wo_all, g_all, b_all, layer, tm=512):
    s, d = x.shape
    m = mem.shape[1]
    n_load = d // LOAD_ROWS
    mem_spec = pl.BlockSpec((None, m, LOAD_ROWS), lambda t: (0, 0, jnp.minimum(t, n_load - 1)))
    wchunk = _weight_chunk(layer, n_load, d)
    return pl.pallas_call(
        functools.partial(_xattn_kernel, n_load=n_load, layer=layer),
        grid=(n_load + s // tm,),
        in_specs=[_seq_tile(n_load, tm, d), mem_spec, wchunk, wchunk, wchunk, wchunk,
                  _whole(g_all), _whole(b_all)],
        out_specs=_seq_tile(n_load, tm, d),
        out_shape=jax.ShapeDtypeStruct((s, d), F32),
        scratch_shapes=[pltpu.VMEM((d, d), BF16), pltpu.VMEM((d, d), BF16),
                        pltpu.VMEM((m, d), F32), pltpu.VMEM((m, d), F32),
                        pltpu.VMEM((m, d), BF16), pltpu.VMEM((m, d), BF16),
                        pltpu.VMEM((tm, d), BF16)],
        compiler_params=_params(("arbitrary",)),
        name="xattn",
    )(x, mem, wq_all, wk_all, wv_all, wo_all, g_all, b_all)


def _ffn_kernel(x_ref, wg_ref, wv_ref, wd_ref, cw_ref, cb_ref, g_ref, b_ref, o_ref,
                xb_ref, carry_ref, gext_ref, *, layer):
    i = pl.program_id(0)
    j = pl.program_id(1)
    tm = x_ref.shape[0]
    lrow = slice(layer, layer + 1)

    @pl.when(j == 0)
    def _():
        xb_ref[...] = x_ref[...].astype(BF16)
        o_ref[...] = jnp.zeros_like(o_ref)

    @pl.when(i == 0)
    def _():
        carry_ref[j] = jnp.zeros(carry_ref.shape[1:], F32)

    xb = xb_ref[...]
    gate = _dot(xb, wg_ref[...])
    val = _dot(xb, wv_ref[...])
    gext_ref[0:SUBLANES, :] = carry_ref[j]
    gext_ref[SUBLANES:SUBLANES + tm, :] = gate
    carry_ref[j] = gate[tm - SUBLANES:tm, :]
    conv = cb_ref[lrow, :] + cw_ref[FFN_CONV - 1:FFN_CONV, :] * gate
    for t in range(FFN_CONV - 1):
        off = SUBLANES - (FFN_CONV - 1) + t
        conv = conv + cw_ref[t:t + 1, :] * gext_ref[off:off + tm, :]
    act = 0.5 * conv * (1.0 + lax.erf(conv * math.sqrt(0.5)))
    o_ref[...] += _dot((act * val).astype(BF16), wd_ref[...])

    @pl.when(j == pl.num_programs(1) - 1)
    def _():
        o_ref[...] = _layer_norm(ALPHA * x_ref[...] + o_ref[...], g_ref[lrow, :], b_ref[lrow, :])


def _ffn(x, wup, wdn, cw_all, cb_all, g_all, b_all, layer, tm=512, tf=512):
    s, d = x.shape
    f = wdn.shape[0]
    nf = f // tf
    nl = cb_all.shape[0]
    return pl.pallas_call(
        functools.partial(_ffn_kernel, layer=layer),
        grid=(s // tm, nf),
        in_specs=[pl.BlockSpec((tm, d), lambda i, j: (i, 0)),
                  pl.BlockSpec((d, tf), lambda i, j: (0, j)),
                  pl.BlockSpec((d, tf), lambda i, j: (0, j + nf)),
                  pl.BlockSpec((tf, d), lambda i, j: (j, 0)),
                  pl.BlockSpec((None, FFN_CONV, tf), lambda i, j: (layer, 0, j)),
                  pl.BlockSpec((nl, tf), lambda i, j: (0, j)),
                  pl.BlockSpec((nl, d), lambda i, j: (0, 0)),
                  pl.BlockSpec((nl, d), lambda i, j: (0, 0))],
        out_specs=pl.BlockSpec((tm, d), lambda i, j: (i, 0)),
        out_shape=jax.ShapeDtypeStruct((s, d), F32),
        scratch_shapes=[pltpu.VMEM((tm, d), BF16),
                        pltpu.VMEM((nf, SUBLANES, tf), F32), pltpu.VMEM((SUBLANES + tm, tf), F32)],
        compiler_params=_params(("arbitrary", "arbitrary")),
        name="ffn",
    )(x, wup, wup, wdn, cw_all, cb_all, g_all, b_all)


def _at_lanes(table, start):
    return jnp.pad(table, ((0, 0), (start, LANES - start - table.shape[1])))


def kernel(x, mem, w_in, gla_w_gate, gla_b_gate, gla_norm_w, ssd_conv_w, ssd_conv_b, ssd_dt_bias, ssd_a_log, ssd_d, ssd_norm_w, w_out, ln_mix_g, ln_mix_b, xa_wq, xa_wk, xa_wv, xa_wo, ln_xa_g, ln_xa_b, ffn_w_up, ffn_conv_w, ffn_conv_b, ffn_w_down, ln_ffn_g, ln_ffn_b):
    assert x.shape == (1, SEQ, D_MODEL) and mem.shape == (1, N_MEM, D_MODEL)
    assert w_in.shape == (DEPTH, D_MODEL, D_IN)
    xs = x[0]
    wg_all = jnp.pad(gla_w_gate, ((0, 0), (0, LANES - GLA_RANK), (0, 0))).astype(BF16)
    dtb_all = _at_lanes(ssd_dt_bias, DT_LANE)
    mixer_tables = (gla_norm_w, _at_lanes(ssd_a_log, DT_LANE), jnp.repeat(ssd_d, SSD_HEADDIM, axis=1), ssd_norm_w)
    w_in_t = jnp.swapaxes(w_in, 1, 2)
    for l in range(DEPTH):
        qkvog, zg, xc, dt, lg = _inproj(xs, w_in_t, wg_all, gla_b_gate, ssd_conv_w, ssd_conv_b, dtb_all, l)
        mixed, wup, wdn = _mixer(qkvog, zg, xc, dt, lg, mixer_tables, ffn_w_up, ffn_w_down, l)
        xs = _proj_ln(mixed, w_out, xs, ln_mix_g, ln_mix_b, l)
        xs = _xattn(xs, mem, xa_wq, xa_wk, xa_wv, xa_wo, ln_xa_g, ln_xa_b, l)
        xs = _ffn(xs, wup, wdn, ffn_conv_w, ffn_conv_b, ln_ffn_g, ln_ffn_b, l)
    return xs[None]
```

```python
import functools
import math

import jax
import jax.numpy as jnp
import numpy as np
from jax import lax
from jax.experimental import pallas as pl
from jax.experimental.pallas import tpu as pltpu

F32 = jnp.float32
BF16 = jnp.bfloat16

D_MODEL = 2048
SEQ = 8192
DEPTH = 4
CHUNK = 64
N_MEM = 256
LN_EPS = 1e-5
RMS_EPS = 1e-6
ALPHA = (2.0 * DEPTH) ** 0.25

GLA_HEADS = 4
GLA_DV = D_MODEL // 2
GLA_HEAD_V = GLA_DV // GLA_HEADS
GLA_DK = GLA_DV // 2
GLA_HEAD_K = GLA_DK // GLA_HEADS
GLA_RANK = 16
GLA_TAU = 16.0
SSD_D_INNER = D_MODEL - GLA_DV
SSD_HEADDIM = 64
SSD_HEADS = SSD_D_INNER // SSD_HEADDIM
SSD_STATE = 128
SSD_GROUPS = 2
SSD_HPG = SSD_HEADS // SSD_GROUPS
SSD_CONV = 4
SSD_BC = SSD_GROUPS * SSD_STATE
SSD_CONV_DIM = SSD_D_INNER + 2 * SSD_BC
SSD_GROUP_W = SSD_D_INNER // SSD_GROUPS
XA_HEADS = 4
XA_HEAD_DIM = D_MODEL // XA_HEADS
D_FF = ((8 * D_MODEL // 3 + 255) // 256) * 256
FFN_CONV = 3

LANES = 128
SUBLANES = 8
QKVOG_W = 2 * GLA_DK + 2 * GLA_DV
SMALL_W = 2 * LANES
D_IN = QKVOG_W + GLA_RANK + SSD_D_INNER + SSD_CONV_DIM + SSD_HEADS
W_IN_PAD = -(-D_IN // LANES) * LANES
W_IN_FLOOR = D_IN // LANES * LANES
Z_END_PAD = -(-(QKVOG_W + GLA_RANK + SSD_D_INNER) // LANES) * LANES
XBC_START_ALIGNED = (QKVOG_W + GLA_RANK + SSD_D_INNER) // LANES * LANES
DT_LANE = D_IN - SSD_HEADS - (W_IN_PAD - LANES)
LOAD_ROWS = 128
INPROJ_LOAD_ROWS = 256
ROW_GROUPS = 4
XA_ROW_GROUPS = 1
VMEM_LIMIT = 56 * 1024 * 1024

NT = (((1,), (1,)), ((), ()))
TN = (((0,), (0,)), ((), ()))


def _dot(a, b):
    return jnp.dot(a, b, preferred_element_type=F32)


def _dotg(a, b, dims):
    return lax.dot_general(a, b, dims, preferred_element_type=F32)


def _split3(v):
    hi = v.astype(BF16)
    r = v - hi.astype(F32)
    mid = r.astype(BF16)
    lo = (r - mid.astype(F32)).astype(BF16)
    return hi, mid, lo


def _exact_left(mat, v):
    hi, mid, lo = _split3(v)
    return _dot(mat, hi) + _dot(mat, mid) + _dot(mat, lo)


def _exact_right(v, mat):
    hi, mid, lo = _split3(v)
    return _dot(hi, mat) + _dot(mid, mat) + _dot(lo, mat)


def _silu(x):
    h = 0.5 * x
    return h + h * jnp.tanh(h)


def _softplus(x):
    return jnp.maximum(x, 0.0) + jnp.log(1.0 + jnp.exp(-jnp.abs(x)))


def _layer_norm(y, g, b):
    mu = jnp.mean(y, axis=-1, keepdims=True)
    d = y - mu
    var = jnp.mean(d * d, axis=-1, keepdims=True)
    return d * lax.rsqrt(var + LN_EPS) * g + b


def _whole(arr):
    nd = arr.ndim
    return pl.BlockSpec(arr.shape, lambda *_: (0,) * nd, pipeline_mode=pl.Buffered(1))


def _params(sem):
    return pltpu.CompilerParams(dimension_semantics=sem, vmem_limit_bytes=VMEM_LIMIT)


def _weight_chunk(layer, n_load, cols):
    return pl.BlockSpec((None, LOAD_ROWS, cols), lambda s: (layer, jnp.minimum(s, n_load - 1), 0))


def _seq_tile(n_load, tm, cols):
    return pl.BlockSpec((tm, cols), lambda s: (jnp.maximum(s - n_load, 0), 0))


def _chunk_rows(step):
    return pl.ds(pl.multiple_of(step * LOAD_ROWS, LOAD_ROWS), LOAD_ROWS)


def _inproj_kernel(x_ref, wt_ref, o1_ref, oz_ref, ox_ref, os_ref, wb_ref, *, n_load):
    s = pl.program_id(0)
    full_rows = (n_load - 1) * INPROJ_LOAD_ROWS
    last_rows = D_IN - full_rows

    @pl.when(s < n_load - 1)
    def _():
        rows = pl.ds(pl.multiple_of(s * INPROJ_LOAD_ROWS, INPROJ_LOAD_ROWS), INPROJ_LOAD_ROWS)
        wb_ref[rows, :] = wt_ref[...].astype(BF16)

    @pl.when(s == n_load - 1)
    def _():
        wb_ref[full_rows:D_IN, :] = wt_ref[0:last_rows, :].astype(BF16)
        wb_ref[D_IN:W_IN_PAD, :] = jnp.zeros((W_IN_PAD - D_IN, wb_ref.shape[1]), BF16)

    @pl.when(s >= n_load)
    def _():
        xb = x_ref[...].astype(BF16)
        o1_ref[...] = _dotg(xb, wb_ref[0:QKVOG_W, :], NT)
        lo = _dotg(xb, wb_ref[QKVOG_W:Z_END_PAD, :], NT)
        os_ref[:, 0:LANES] = lo[:, 0:LANES]
        oz_ref[...] = lo[:, GLA_RANK:GLA_RANK + SSD_D_INNER]
        hi = _dotg(xb, wb_ref[XBC_START_ALIGNED:W_IN_PAD, :], NT)
        ox_ref[...] = hi[:, GLA_RANK:GLA_RANK + SSD_CONV_DIM]
        os_ref[:, LANES:2 * LANES] = hi[:, W_IN_PAD - XBC_START_ALIGNED - LANES:W_IN_PAD - XBC_START_ALIGNED]


def _inproj(x, w_in_t, layer, tm=256):
    s, d = x.shape
    n_load = -(-D_IN // INPROJ_LOAD_ROWS)
    assert (D_IN - (n_load - 1) * INPROJ_LOAD_ROWS) % (2 * SUBLANES) == 0
    widths = (QKVOG_W, SSD_D_INNER, SSD_CONV_DIM, SMALL_W)
    wt_spec = pl.BlockSpec((None, INPROJ_LOAD_ROWS, d), lambda t: (layer, jnp.minimum(t, n_load - 1), 0))
    return pl.pallas_call(
        functools.partial(_inproj_kernel, n_load=n_load),
        grid=(n_load + s // tm,),
        in_specs=[_seq_tile(n_load, tm, d), wt_spec],
        out_specs=[_seq_tile(n_load, tm, n) for n in widths],
        out_shape=[jax.ShapeDtypeStruct((s, n), F32) for n in widths],
        scratch_shapes=[pltpu.VMEM((W_IN_PAD, d), BF16)],
        compiler_params=_params(("arbitrary",)),
        name="inproj",
    )(x, w_in_t)


def _mixer_kernel(qkvog_ref, z_ref, xbc_ref, sm_ref, wg_ref, bg_ref, gnw_ref, cw_ref, cb_ref,
                  dtb_ref, a_ref, dexp_ref, snw_ref, e_ref, tri_ref, trile_ref, diage_ref, bd_ref,
                  wup_ref, wdn_ref, out_ref, wup_b_ref, wdn_b_ref, sgla_ref, sssd_ref, xprev_ref, *, layer):
    c = pl.program_id(0)
    lrow = slice(layer, layer + 1)
    wup_b_ref[...] = wup_ref[...].astype(BF16)
    wdn_b_ref[...] = wdn_ref[...].astype(BF16)

    @pl.when(c == 0)
    def _():
        sgla_ref[...] = jnp.zeros_like(sgla_ref)
        sssd_ref[...] = jnp.zeros_like(sssd_ref)
        xprev_ref[...] = jnp.zeros_like(xprev_ref)

    refs = (qkvog_ref, z_ref, xbc_ref, sm_ref, wg_ref, bg_ref, gnw_ref, cw_ref, cb_ref, dtb_ref, a_ref,
            dexp_ref, snw_ref, e_ref, tri_ref, trile_ref, diage_ref, bd_ref, out_ref, sgla_ref, sssd_ref)
    tail = xprev_ref[...]
    for ci in range(qkvog_ref.shape[0] // CHUNK):
        rows = slice(ci * CHUNK, (ci + 1) * CHUNK)
        _mixer_chunk(refs, rows, tail, lrow)
        tail = xbc_ref[(ci + 1) * CHUNK - SUBLANES:(ci + 1) * CHUNK, :]
    xprev_ref[...] = tail


def _mixer_chunk(refs, rows, tail, lrow):
    (qkvog_ref, z_ref, xbc_ref, sm_ref, wg_ref, bg_ref, gnw_ref, cw_ref, cb_ref, dtb_ref, a_ref,
     dexp_ref, snw_ref, e_ref, tri_ref, trile_ref, diage_ref, bd_ref, out_ref, sgla_ref, sssd_ref) = refs
    tri = tri_ref[...]
    tri_mask = tri > 0

    alr = sm_ref[rows, 0:LANES].astype(BF16)
    gpre = _dot(alr, wg_ref[...]) + bg_ref[lrow, :]
    log_g = -_softplus(-gpre) / GLA_TAU
    b = _exact_left(tri, log_g)
    b_last = b[CHUNK - 1:CHUNK, :]
    q = qkvog_ref[rows, 0:GLA_DK]
    k = qkvog_ref[rows, GLA_DK:2 * GLA_DK]
    q_dec = ((q * (GLA_HEAD_K ** -0.5)) * jnp.exp(b)).astype(BF16)
    k_inv = (k * jnp.exp(-b)).astype(BF16)
    k_end = (k * jnp.exp(b_last - b)).astype(BF16)
    e_last = jnp.exp(b_last)
    for h in range(GLA_HEADS):
        ks = slice(h * GLA_HEAD_K, (h + 1) * GLA_HEAD_K)
        vs = slice(2 * GLA_DK + h * GLA_HEAD_V, 2 * GLA_DK + (h + 1) * GLA_HEAD_V)
        gs = slice(2 * GLA_DK + GLA_DV + h * GLA_HEAD_V, 2 * GLA_DK + GLA_DV + (h + 1) * GLA_HEAD_V)
        os_ = slice(h * GLA_HEAD_V, (h + 1) * GLA_HEAD_V)
        vh = qkvog_ref[rows, vs].astype(BF16)
        scores = jnp.where(tri_mask, _dotg(q_dec[:, ks], k_inv[:, ks], NT), 0.0)
        state_t = sgla_ref[h]
        o = _dot(scores.astype(BF16), vh) + _dotg(q_dec[:, ks], state_t.astype(BF16), NT)
        sgla_ref[h] = e_last[:, ks] * state_t + _dotg(vh, k_end[:, ks], TN)
        o = o * lax.rsqrt(jnp.mean(o * o, axis=-1, keepdims=True) + RMS_EPS) * gnw_ref[lrow, os_]
        og = qkvog_ref[rows, gs]
        out_ref[rows, os_] = (o * _silu(og)).astype(BF16)

    xbc = xbc_ref[rows, :]
    xext = jnp.concatenate([tail, xbc], axis=0)
    conv = cb_ref[lrow, :] + cw_ref[SSD_CONV - 1:SSD_CONV, :] * xbc
    for back in range(1, SSD_CONV):
        shifted = pltpu.roll(xext, back, axis=0)[SUBLANES:SUBLANES + CHUNK, :]
        conv = conv + cw_ref[SSD_CONV - 1 - back:SSD_CONV - back, :] * shifted
    xc = _silu(conv)
    xs = xc[:, 0:SSD_D_INNER]
    dt = _softplus(sm_ref[rows, LANES:2 * LANES] + dtb_ref[lrow, :])
    cs = _exact_left(tri, dt * -jnp.exp(a_ref[lrow, :]))
    expand = e_ref[...]
    dt_e = _exact_right(dt[:, 0:DT_LANE + SSD_HEADS], expand)
    cs_e = _exact_right(cs[:, 0:DT_LANE + SSD_HEADS], expand)
    cs_last_e = cs_e[CHUNK - 1:CHUNK, :]
    xdt = xs * dt_e
    cs_row = jnp.sum(cs_e * diage_ref[...], axis=0, keepdims=True)
    lmat = jnp.where(trile_ref[...] > 0, jnp.exp(cs_e - cs_row), 0.0)
    xdt_end = (xdt * jnp.exp(cs_last_e - cs_e)).astype(BF16)
    xdt_b = xdt.astype(BF16)
    decay_in = jnp.exp(cs_e)
    decay_chunk = jnp.exp(cs_last_e)
    bd = bd_ref[...] > 0
    for g in range(SSD_GROUPS):
        gl = slice(g * SSD_GROUP_W, (g + 1) * SSD_GROUP_W)
        bm = xc[:, SSD_D_INNER + g * SSD_STATE:SSD_D_INNER + (g + 1) * SSD_STATE].astype(BF16)
        cm = xc[:, SSD_D_INNER + SSD_BC + g * SSD_STATE:
                SSD_D_INNER + SSD_BC + (g + 1) * SSD_STATE].astype(BF16)
        bm_rep = jnp.concatenate([bm] * SSD_HPG, axis=0)
        w = (_dotg(cm, bm_rep, NT) * lmat[:, gl]).astype(BF16)
        state_t = sssd_ref[g]
        y_off = _dot(cm, state_t.astype(BF16)) * decay_in[:, gl]
        sssd_ref[g] = decay_chunk[:, gl] * state_t + _dotg(bm, xdt_end[:, gl], TN)
        y_diag = []
        for hp in range(SSD_HPG // 2):
            pl_ = slice(g * SSD_GROUP_W + hp * 2 * SSD_HEADDIM, g * SSD_GROUP_W + (hp + 1) * 2 * SSD_HEADDIM)
            wl = slice(hp * 2 * CHUNK, (hp + 1) * 2 * CHUNK)
            x2 = xdt_b[:, pl_]
            x_bd = jnp.where(bd, jnp.concatenate([x2, x2], axis=0), jnp.zeros((), BF16))
            y_diag.append(_dot(w[:, wl], x_bd))
        y = jnp.concatenate(y_diag, axis=1) + y_off + dexp_ref[lrow, gl] * xs[:, gl]
        zg = z_ref[rows, gl]
        y = y * _silu(zg)
        y = y * lax.rsqrt(jnp.mean(y * y, axis=-1, keepdims=True) + RMS_EPS) * snw_ref[lrow, gl]
        out_ref[rows, GLA_DV + g * SSD_GROUP_W:GLA_DV + (g + 1) * SSD_GROUP_W] = y.astype(BF16)


def _mixer_constants():
    l = np.arange(CHUNK)
    tri = (l[None, :] <= l[:, None]).astype(np.float32)
    lane = np.arange(SSD_D_INNER)
    m_of_lane = lane % SSD_HEADDIM
    trile = (m_of_lane[None, :] <= l[:, None]).astype(np.float32)
    diage = (m_of_lane[None, :] == l[:, None]).astype(np.float32)
    expand = np.zeros((DT_LANE + SSD_HEADS, SSD_D_INNER), np.float32)
    expand[DT_LANE + lane // SSD_HEADDIM, lane] = 1.0
    r = np.arange(2 * SSD_HEADDIM)
    bd = ((r[:, None] // SSD_HEADDIM) == (r[None, :] // SSD_HEADDIM)).astype(np.float32)
    return (jnp.asarray(expand, BF16), jnp.asarray(tri, BF16), jnp.asarray(trile), jnp.asarray(diage),
            jnp.asarray(bd))


def _mixer(qkvog, z, xbc, sm, wg_all, per_layer, ffn_w_up, ffn_w_down, layer, chunks_per_step=4):
    assert CHUNK == SSD_HEADDIM
    s = qkvog.shape[0]
    consts = _mixer_constants()
    tm = chunks_per_step * CHUNK
    steps = s // tm
    row = lambda n: pl.BlockSpec((tm, n), lambda i: (i, 0))
    bg, gnw, cw, cb, dtb, a_log, dexp, snw = per_layer
    layer_slab = lambda t: pl.BlockSpec((None,) + t.shape[1:], lambda i: (layer, 0, 0))
    order = (wg_all, bg, gnw, cw, cb, dtb, a_log, dexp, snw) + consts
    in_specs = [row(QKVOG_W), row(SSD_D_INNER), row(SSD_CONV_DIM), row(SMALL_W),
                layer_slab(wg_all), _whole(bg), _whole(gnw), layer_slab(cw)]
    in_specs += [_whole(t) for t in order[4:]]
    up_rows, up_cols = ffn_w_up.shape[1] // steps, ffn_w_up.shape[2]
    dn_rows, dn_cols = ffn_w_down.shape[1] // steps, ffn_w_down.shape[2]
    assert up_rows * steps == ffn_w_up.shape[1] and up_rows % (2 * SUBLANES) == 0
    assert dn_rows * steps == ffn_w_down.shape[1] and dn_rows % (2 * SUBLANES) == 0
    in_specs += [pl.BlockSpec((None, up_rows, up_cols), lambda i: (layer, i, 0)),
                 pl.BlockSpec((None, dn_rows, dn_cols), lambda i: (layer, i, 0))]
    return pl.pallas_call(
        functools.partial(_mixer_kernel, layer=layer),
        grid=(steps,),
        in_specs=in_specs,
        out_specs=[row(D_MODEL), pl.BlockSpec((up_rows, up_cols), lambda i: (i, 0)),
                   pl.BlockSpec((dn_rows, dn_cols), lambda i: (i, 0))],
        out_shape=[jax.ShapeDtypeStruct((s, D_MODEL), BF16),
                   jax.ShapeDtypeStruct(ffn_w_up.shape[1:], BF16),
                   jax.ShapeDtypeStruct(ffn_w_down.shape[1:], BF16)],
        scratch_shapes=[
            pltpu.VMEM((GLA_HEADS, GLA_HEAD_V, GLA_HEAD_K), F32),
            pltpu.VMEM((SSD_GROUPS, SSD_STATE, SSD_GROUP_W), F32),
            pltpu.VMEM((SUBLANES, SSD_CONV_DIM), F32),
        ],
        compiler_params=_params(("arbitrary",)),
        name="mixer",
    )(qkvog, z, xbc, sm, *order, ffn_w_up, ffn_w_down)


def _proj_ln_kernel(a_ref, w_ref, x_ref, g_ref, b_ref, o_ref, wb_ref, *, n_load, layer):
    s = pl.program_id(0)
    lrow = slice(layer, layer + 1)

    @pl.when(s < n_load)
    def _():
        wb_ref[_chunk_rows(s), :] = w_ref[...].astype(BF16)

    @pl.when(s >= n_load)
    def _():
        half = a_ref.shape[0] // ROW_GROUPS
        for r in range(ROW_GROUPS):
            rows = slice(r * half, (r + 1) * half)
            h = _dot(a_ref[rows, :], wb_ref[...])
            o_ref[rows, :] = _layer_norm(ALPHA * x_ref[rows, :] + h, g_ref[lrow, :], b_ref[lrow, :])


def _proj_ln(a, w_all, x, g_all, b_all, layer, tm=512):
    s, d = x.shape
    kdim = a.shape[1]
    n_load = kdim // LOAD_ROWS
    return pl.pallas_call(
        functools.partial(_proj_ln_kernel, n_load=n_load, layer=layer),
        grid=(n_load + s // tm,),
        in_specs=[_seq_tile(n_load, tm, kdim), _weight_chunk(layer, n_load, d), _seq_tile(n_load, tm, d),
                  _whole(g_all), _whole(b_all)],
        out_specs=_seq_tile(n_load, tm, d),
        out_shape=jax.ShapeDtypeStruct((s, d), F32),
        scratch_shapes=[pltpu.VMEM((kdim, d), BF16)],
        compiler_params=_params(("arbitrary",)),
        name="proj_ln",
    )(a, w_all, x, g_all, b_all)


def _xattn_kernel(x_ref, mem_ref, wq_ref, wk_ref, wv_ref, wo_ref, g_ref, b_ref, o_ref,
                  wqb_ref, wob_ref, kacc_ref, vacc_ref, kb_ref, vb_ref, att_ref, *, n_load, layer):
    s = pl.program_id(0)
    lrow = slice(layer, layer + 1)

    @pl.when(s == 0)
    def _():
        kacc_ref[...] = jnp.zeros_like(kacc_ref)
        vacc_ref[...] = jnp.zeros_like(vacc_ref)

    @pl.when(s < n_load)
    def _():
        rows = _chunk_rows(s)
        wqb_ref[rows, :] = wq_ref[...].astype(BF16)
        wob_ref[rows, :] = wo_ref[...].astype(BF16)
        mb = mem_ref[...].astype(BF16)
        kacc_ref[...] += _dot(mb, wk_ref[...].astype(BF16))
        vacc_ref[...] += _dot(mb, wv_ref[...].astype(BF16))

    @pl.when(s == n_load - 1)
    def _():
        kb_ref[...] = kacc_ref[...].astype(BF16)
        vb_ref[...] = vacc_ref[...].astype(BF16)

    @pl.when(s >= n_load)
    def _():
        group = x_ref.shape[0] // XA_ROW_GROUPS
        for r in range(XA_ROW_GROUPS):
            rows = slice(r * group, (r + 1) * group)
            x = x_ref[rows, :]
            q = _dot(x.astype(BF16), wqb_ref[...])
            for h in range(XA_HEADS):
                hs = slice(h * XA_HEAD_DIM, (h + 1) * XA_HEAD_DIM)
                sc = _dotg(q[:, hs].astype(BF16), kb_ref[:, hs], NT) * (XA_HEAD_DIM ** -0.5)
                e = jnp.exp(sc - jnp.max(sc, axis=-1, keepdims=True))
                p = e / jnp.sum(e, axis=-1, keepdims=True)
                att_ref[rows, hs] = _dot(p.astype(BF16), vb_ref[:, hs]).astype(BF16)
            h_out = _dot(att_ref[rows, :], wob_ref[...])
            o_ref[rows, :] = _layer_norm(ALPHA * x + h_out, g_ref[lrow, :], b_ref[lrow, :])


def _xattn(x, mem, wq_all, wk_all, wv_all, wo_all, g_all, b_all, layer, tm=512):
    s, d = x.shape
    m = mem.shape[1]
    n_load = d // LOAD_ROWS
    mem_spec = pl.BlockSpec((None, m, LOAD_ROWS), lambda t: (0, 0, jnp.minimum(t, n_load - 1)))
    wchunk = _weight_chunk(layer, n_load, d)
    return pl.pallas_call(
        functools.partial(_xattn_kernel, n_load=n_load, layer=layer),
        grid=(n_load + s // tm,),
        in_specs=[_seq_tile(n_load, tm, d), mem_spec, wchunk, wchunk, wchunk, wchunk,
                  _whole(g_all), _whole(b_all)],
        out_specs=_seq_tile(n_load, tm, d),
        out_shape=jax.ShapeDtypeStruct((s, d), F32),
        scratch_shapes=[pltpu.VMEM((d, d), BF16), pltpu.VMEM((d, d), BF16),
                        pltpu.VMEM((m, d), F32), pltpu.VMEM((m, d), F32),
                        pltpu.VMEM((m, d), BF16), pltpu.VMEM((m, d), BF16),
                        pltpu.VMEM((tm, d), BF16)],
        compiler_params=_params(("arbitrary",)),
        name="xattn",
    )(x, mem, wq_all, wk_all, wv_all, wo_all, g_all, b_all)


def _ffn_kernel(x_ref, wg_ref, wv_ref, wd_ref, cw_ref, cb_ref, g_ref, b_ref, o_ref,
                xb_ref, carry_ref, gext_ref, *, layer):
    i = pl.program_id(0)
    j = pl.program_id(1)
    tm = x_ref.shape[0]
    lrow = slice(layer, layer + 1)

    @pl.when(j == 0)
    def _():
        xb_ref[...] = x_ref[...].astype(BF16)
        o_ref[...] = jnp.zeros_like(o_ref)

    @pl.when(i == 0)
    def _():
        carry_ref[j] = jnp.zeros(carry_ref.shape[1:], F32)

    xb = xb_ref[...]
    gate = _dot(xb, wg_ref[...])
    val = _dot(xb, wv_ref[...])
    gext_ref[0:SUBLANES, :] = carry_ref[j]
    gext_ref[SUBLANES:SUBLANES + tm, :] = gate
    carry_ref[j] = gate[tm - SUBLANES:tm, :]
    conv = cb_ref[lrow, :] + cw_ref[FFN_CONV - 1:FFN_CONV, :] * gate
    for t in range(FFN_CONV - 1):
        off = SUBLANES - (FFN_CONV - 1) + t
        conv = conv + cw_ref[t:t + 1, :] * gext_ref[off:off + tm, :]
    act = 0.5 * conv * (1.0 + lax.erf(conv * math.sqrt(0.5)))
    o_ref[...] += _dot((act * val).astype(BF16), wd_ref[...])

    @pl.when(j == pl.num_programs(1) - 1)
    def _():
        o_ref[...] = _layer_norm(ALPHA * x_ref[...] + o_ref[...], g_ref[lrow, :], b_ref[lrow, :])


def _ffn(x, wup, wdn, cw_all, cb_all, g_all, b_all, layer, tm=512, tf=512):
    s, d = x.shape
    f = wdn.shape[0]
    nf = f // tf
    nl = cb_all.shape[0]
    return pl.pallas_call(
        functools.partial(_ffn_kernel, layer=layer),
        grid=(s // tm, nf),
        in_specs=[pl.BlockSpec((tm, d), lambda i, j: (i, 0)),
                  pl.BlockSpec((d, tf), lambda i, j: (0, j)),
                  pl.BlockSpec((d, tf), lambda i, j: (0, j + nf)),
                  pl.BlockSpec((tf, d), lambda i, j: (j, 0)),
                  pl.BlockSpec((None, FFN_CONV, tf), lambda i, j: (layer, 0, j)),
                  pl.BlockSpec((nl, tf), lambda i, j: (0, j)),
                  pl.BlockSpec((nl, d), lambda i, j: (0, 0)),
                  pl.BlockSpec((nl, d), lambda i, j: (0, 0))],
        out_specs=pl.BlockSpec((tm, d), lambda i, j: (i, 0)),
        out_shape=jax.ShapeDtypeStruct((s, d), F32),
        scratch_shapes=[pltpu.VMEM((tm, d), BF16),
                        pltpu.VMEM((nf, SUBLANES, tf), F32), pltpu.VMEM((SUBLANES + tm, tf), F32)],
        compiler_params=_params(("arbitrary", "arbitrary")),
        name="ffn",
    )(x, wup, wup, wdn, cw_all, cb_all, g_all, b_all)


def _at_lanes(table, start):
    return jnp.pad(table, ((0, 0), (start, LANES - start - table.shape[1])))


def kernel(x, mem, w_in, gla_w_gate, gla_b_gate, gla_norm_w, ssd_conv_w, ssd_conv_b, ssd_dt_bias, ssd_a_log, ssd_d, ssd_norm_w, w_out, ln_mix_g, ln_mix_b, xa_wq, xa_wk, xa_wv, xa_wo, ln_xa_g, ln_xa_b, ffn_w_up, ffn_conv_w, ffn_conv_b, ffn_w_down, ln_ffn_g, ln_ffn_b):
    assert x.shape == (1, SEQ, D_MODEL) and mem.shape == (1, N_MEM, D_MODEL)
    assert w_in.shape == (DEPTH, D_MODEL, D_IN)
    xs = x[0]
    wg_all = jnp.pad(gla_w_gate, ((0, 0), (0, LANES - GLA_RANK), (0, 0))).astype(BF16)
    mixer_tables = (gla_b_gate, gla_norm_w, ssd_conv_w, ssd_conv_b, _at_lanes(ssd_dt_bias, DT_LANE),
                    _at_lanes(ssd_a_log, DT_LANE), jnp.repeat(ssd_d, SSD_HEADDIM, axis=1), ssd_norm_w)
    w_in_t = jnp.swapaxes(w_in, 1, 2)
    for l in range(DEPTH):
        qkvog, z, xbc, sm = _inproj(xs, w_in_t, l)
        mixed, wup, wdn = _mixer(qkvog, z, xbc, sm, wg_all, mixer_tables, ffn_w_up, ffn_w_down, l)
        xs = _proj_ln(mixed, w_out, xs, ln_mix_g, ln_mix_b, l)
        xs = _xattn(xs, mem, xa_wq, xa_wk, xa_wv, xa_wo, ln_xa_g, ln_xa_b, l)
        xs = _ffn(xs, wup, wdn, ffn_conv_w, ffn_conv_b, ln_ffn_g, ln_ffn_b, l)
    return xs[None]
```

```python
import functools
import math

import jax
import jax.numpy as jnp
import numpy as np
from jax import lax
from jax.experimental import pallas as pl
from jax.experimental.pallas import tpu as pltpu

F32 = jnp.float32
BF16 = jnp.bfloat16

D_MODEL = 2048
SEQ = 8192
DEPTH = 4
CHUNK = 64
N_MEM = 256
LN_EPS = 1e-5
RMS_EPS = 1e-6
ALPHA = (2.0 * DEPTH) ** 0.25

GLA_HEADS = 4
GLA_DV = D_MODEL // 2
GLA_HEAD_V = GLA_DV // GLA_HEADS
GLA_DK = GLA_DV // 2
GLA_HEAD_K = GLA_DK // GLA_HEADS
GLA_RANK = 16
GLA_TAU = 16.0
SSD_D_INNER = D_MODEL - GLA_DV
SSD_HEADDIM = 64
SSD_HEADS = SSD_D_INNER // SSD_HEADDIM
SSD_STATE = 128
SSD_GROUPS = 2
SSD_HPG = SSD_HEADS // SSD_GROUPS
SSD_CONV = 4
SSD_BC = SSD_GROUPS * SSD_STATE
SSD_CONV_DIM = SSD_D_INNER + 2 * SSD_BC
SSD_GROUP_W = SSD_D_INNER // SSD_GROUPS
XA_HEADS = 4
XA_HEAD_DIM = D_MODEL // XA_HEADS
D_FF = ((8 * D_MODEL // 3 + 255) // 256) * 256
FFN_CONV = 3

LANES = 128
SUBLANES = 8
QKVOG_W = 2 * GLA_DK + 2 * GLA_DV
SMALL_W = 2 * LANES
D_IN = QKVOG_W + GLA_RANK + SSD_D_INNER + SSD_CONV_DIM + SSD_HEADS
W_IN_PAD = -(-D_IN // LANES) * LANES
Z_END_PAD = -(-(QKVOG_W + GLA_RANK + SSD_D_INNER) // LANES) * LANES
XBC_START_ALIGNED = (QKVOG_W + GLA_RANK + SSD_D_INNER) // LANES * LANES
DT_LANE = D_IN - SSD_HEADS - (W_IN_PAD - LANES)
LOAD_ROWS = 128
INPROJ_LOAD_ROWS = 256
ROW_GROUPS = 4
VMEM_LIMIT = 56 * 1024 * 1024

NT = (((1,), (1,)), ((), ()))
TN = (((0,), (0,)), ((), ()))


def _dot(a, b):
    return jnp.dot(a, b, preferred_element_type=F32)


def _dotg(a, b, dims):
    return lax.dot_general(a, b, dims, preferred_element_type=F32)


def _split3(v):
    hi = v.astype(BF16)
    r = v - hi.astype(F32)
    mid = r.astype(BF16)
    lo = (r - mid.astype(F32)).astype(BF16)
    return hi, mid, lo


def _exact_left(mat, v):
    hi, mid, lo = _split3(v)
    return _dot(mat, hi) + _dot(mat, mid) + _dot(mat, lo)


def _exact_right(v, mat):
    hi, mid, lo = _split3(v)
    return _dot(hi, mat) + _dot(mid, mat) + _dot(lo, mat)


def _silu(x):
    h = 0.5 * x
    return h + h * jnp.tanh(h)


def _softplus(x):
    return jnp.maximum(x, 0.0) + jnp.log(1.0 + jnp.exp(-jnp.abs(x)))


def _layer_norm(y, g, b):
    mu = jnp.mean(y, axis=-1, keepdims=True)
    d = y - mu
    var = jnp.mean(d * d, axis=-1, keepdims=True)
    return d * lax.rsqrt(var + LN_EPS) * g + b


def _whole(arr):
    nd = arr.ndim
    return pl.BlockSpec(arr.shape, lambda *_: (0,) * nd, pipeline_mode=pl.Buffered(1))


def _params(sem):
    return pltpu.CompilerParams(dimension_semantics=sem, vmem_limit_bytes=VMEM_LIMIT)


def _weight_chunk(layer, n_load, cols):
    return pl.BlockSpec((None, LOAD_ROWS, cols), lambda s: (layer, jnp.minimum(s, n_load - 1), 0))


def _seq_tile(n_load, tm, cols):
    return pl.BlockSpec((tm, cols), lambda s: (jnp.maximum(s - n_load, 0), 0))


def _chunk_rows(step):
    return pl.ds(pl.multiple_of(step * LOAD_ROWS, LOAD_ROWS), LOAD_ROWS)


def _inproj_kernel(x_ref, wt_ref, o1_ref, oz_ref, ox_ref, os_ref, wb_ref, *, n_load):
    s = pl.program_id(0)
    full_rows = (n_load - 1) * INPROJ_LOAD_ROWS
    last_rows = D_IN - full_rows

    @pl.when(s < n_load - 1)
    def _():
        rows = pl.ds(pl.multiple_of(s * INPROJ_LOAD_ROWS, INPROJ_LOAD_ROWS), INPROJ_LOAD_ROWS)
        wb_ref[rows, :] = wt_ref[...].astype(BF16)

    @pl.when(s == n_load - 1)
    def _():
        wb_ref[full_rows:D_IN, :] = wt_ref[0:last_rows, :].astype(BF16)
        wb_ref[D_IN:W_IN_PAD, :] = jnp.zeros((W_IN_PAD - D_IN, wb_ref.shape[1]), BF16)

    @pl.when(s >= n_load)
    def _():
        xb = x_ref[...].astype(BF16)
        o1_ref[...] = _dotg(xb, wb_ref[0:QKVOG_W, :], NT)
        lo = _dotg(xb, wb_ref[QKVOG_W:Z_END_PAD, :], NT)
        os_ref[:, 0:LANES] = lo[:, 0:LANES]
        oz_ref[...] = lo[:, GLA_RANK:GLA_RANK + SSD_D_INNER]
        hi = _dotg(xb, wb_ref[XBC_START_ALIGNED:W_IN_PAD, :], NT)
        ox_ref[...] = hi[:, GLA_RANK:GLA_RANK + SSD_CONV_DIM]
        os_ref[:, LANES:2 * LANES] = hi[:, W_IN_PAD - XBC_START_ALIGNED - LANES:W_IN_PAD - XBC_START_ALIGNED]


def _inproj(x, w_in_t, layer, tm=256):
    s, d = x.shape
    n_load = -(-D_IN // INPROJ_LOAD_ROWS)
    assert (D_IN - (n_load - 1) * INPROJ_LOAD_ROWS) % (2 * SUBLANES) == 0
    widths = (QKVOG_W, SSD_D_INNER, SSD_CONV_DIM, SMALL_W)
    wt_spec = pl.BlockSpec((None, INPROJ_LOAD_ROWS, d), lambda t: (layer, jnp.minimum(t, n_load - 1), 0))
    return pl.pallas_call(
        functools.partial(_inproj_kernel, n_load=n_load),
        grid=(n_load + s // tm,),
        in_specs=[_seq_tile(n_load, tm, d), wt_spec],
        out_specs=[_seq_tile(n_load, tm, n) for n in widths],
        out_shape=[jax.ShapeDtypeStruct((s, n), F32) for n in widths],
        scratch_shapes=[pltpu.VMEM((W_IN_PAD, d), BF16)],
        compiler_params=_params(("arbitrary",)),
        name="inproj",
    )(x, w_in_t)


def _mixer_kernel(qkvog_ref, z_ref, xbc_ref, sm_ref, wg_ref, bg_ref, gnw_ref, cw_ref, cb_ref,
                  dtb_ref, a_ref, dexp_ref, snw_ref, e_ref, tri_ref, trile_ref, diage_ref, bd_ref,
                  wup_ref, wdn_ref, out_ref, wup_b_ref, wdn_b_ref, sgla_ref, sssd_ref, xprev_ref, *, layer):
    c = pl.program_id(0)
    lrow = slice(layer, layer + 1)
    wup_b_ref[...] = wup_ref[...].astype(BF16)
    wdn_b_ref[...] = wdn_ref[...].astype(BF16)

    @pl.when(c == 0)
    def _():
        sgla_ref[...] = jnp.zeros_like(sgla_ref)
        sssd_ref[...] = jnp.zeros_like(sssd_ref)
        xprev_ref[...] = jnp.zeros_like(xprev_ref)

    refs = (qkvog_ref, z_ref, xbc_ref, sm_ref, wg_ref, bg_ref, gnw_ref, cw_ref, cb_ref, dtb_ref, a_ref,
            dexp_ref, snw_ref, e_ref, tri_ref, trile_ref, diage_ref, bd_ref, out_ref, sgla_ref, sssd_ref)
    tail = xprev_ref[...]
    for ci in range(qkvog_ref.shape[0] // CHUNK):
        rows = slice(ci * CHUNK, (ci + 1) * CHUNK)
        _mixer_chunk(refs, rows, tail, lrow)
        tail = xbc_ref[(ci + 1) * CHUNK - SUBLANES:(ci + 1) * CHUNK, :]
    xprev_ref[...] = tail


def _mixer_chunk(refs, rows, tail, lrow):
    (qkvog_ref, z_ref, xbc_ref, sm_ref, wg_ref, bg_ref, gnw_ref, cw_ref, cb_ref, dtb_ref, a_ref,
     dexp_ref, snw_ref, e_ref, tri_ref, trile_ref, diage_ref, bd_ref, out_ref, sgla_ref, sssd_ref) = refs
    tri = tri_ref[...]
    tri_mask = tri > 0

    alr = sm_ref[rows, 0:LANES].astype(BF16)
    gpre = _dot(alr, wg_ref[...]) + bg_ref[lrow, :]
    log_g = -_softplus(-gpre) / GLA_TAU
    b = _exact_left(tri, log_g)
    b_last = b[CHUNK - 1:CHUNK, :]
    q = qkvog_ref[rows, 0:GLA_DK]
    k = qkvog_ref[rows, GLA_DK:2 * GLA_DK]
    q_dec = ((q * (GLA_HEAD_K ** -0.5)) * jnp.exp(b)).astype(BF16)
    k_inv = (k * jnp.exp(-b)).astype(BF16)
    k_end = (k * jnp.exp(b_last - b)).astype(BF16)
    e_last = jnp.exp(b_last)
    for h in range(GLA_HEADS):
        ks = slice(h * GLA_HEAD_K, (h + 1) * GLA_HEAD_K)
        vs = slice(2 * GLA_DK + h * GLA_HEAD_V, 2 * GLA_DK + (h + 1) * GLA_HEAD_V)
        gs = slice(2 * GLA_DK + GLA_DV + h * GLA_HEAD_V, 2 * GLA_DK + GLA_DV + (h + 1) * GLA_HEAD_V)
        os_ = slice(h * GLA_HEAD_V, (h + 1) * GLA_HEAD_V)
        vh = qkvog_ref[rows, vs].astype(BF16)
        scores = jnp.where(tri_mask, _dotg(q_dec[:, ks], k_inv[:, ks], NT), 0.0)
        state_t = sgla_ref[h]
        o = _dot(scores.astype(BF16), vh) + _dotg(q_dec[:, ks], state_t.astype(BF16), NT)
        sgla_ref[h] = e_last[:, ks] * state_t + _dotg(vh, k_end[:, ks], TN)
        o = o * lax.rsqrt(jnp.mean(o * o, axis=-1, keepdims=True) + RMS_EPS) * gnw_ref[lrow, os_]
        og = qkvog_ref[rows, gs]
        out_ref[rows, os_] = (o * _silu(og)).astype(BF16)

    xbc = xbc_ref[rows, :]
    xext = jnp.concatenate([tail, xbc], axis=0)
    conv = cb_ref[lrow, :] + cw_ref[SSD_CONV - 1:SSD_CONV, :] * xbc
    for back in range(1, SSD_CONV):
        shifted = pltpu.roll(xext, back, axis=0)[SUBLANES:SUBLANES + CHUNK, :]
        conv = conv + cw_ref[SSD_CONV - 1 - back:SSD_CONV - back, :] * shifted
    xc = _silu(conv)
    xs = xc[:, 0:SSD_D_INNER]
    dt = _softplus(sm_ref[rows, LANES:2 * LANES] + dtb_ref[lrow, :])
    cs = _exact_left(tri, dt * -jnp.exp(a_ref[lrow, :]))
    expand = e_ref[...]
    dt_e = _exact_right(dt[:, 0:DT_LANE + SSD_HEADS], expand)
    cs_e = _exact_right(cs[:, 0:DT_LANE + SSD_HEADS], expand)
    cs_last_e = cs_e[CHUNK - 1:CHUNK, :]
    xdt = xs * dt_e
    cs_row = jnp.sum(cs_e * diage_ref[...], axis=0, keepdims=True)
    lmat = jnp.where(trile_ref[...] > 0, jnp.exp(cs_e - cs_row), 0.0)
    xdt_end = (xdt * jnp.exp(cs_last_e - cs_e)).astype(BF16)
    xdt_b = xdt.astype(BF16)
    decay_in = jnp.exp(cs_e)
    decay_chunk = jnp.exp(cs_last_e)
    bd = bd_ref[...] > 0
    for g in range(SSD_GROUPS):
        gl = slice(g * SSD_GROUP_W, (g + 1) * SSD_GROUP_W)
        bm = xc[:, SSD_D_INNER + g * SSD_STATE:SSD_D_INNER + (g + 1) * SSD_STATE].astype(BF16)
        cm = xc[:, SSD_D_INNER + SSD_BC + g * SSD_STATE:
                SSD_D_INNER + SSD_BC + (g + 1) * SSD_STATE].astype(BF16)
        bm_rep = jnp.concatenate([bm] * SSD_HPG, axis=0)
        w = (_dotg(cm, bm_rep, NT) * lmat[:, gl]).astype(BF16)
        state_t = sssd_ref[g]
        y_off = _dot(cm, state_t.astype(BF16)) * decay_in[:, gl]
        sssd_ref[g] = decay_chunk[:, gl] * state_t + _dotg(bm, xdt_end[:, gl], TN)
        y_diag = []
        for hp in range(SSD_HPG // 2):
            pl_ = slice(g * SSD_GROUP_W + hp * 2 * SSD_HEADDIM, g * SSD_GROUP_W + (hp + 1) * 2 * SSD_HEADDIM)
            wl = slice(hp * 2 * CHUNK, (hp + 1) * 2 * CHUNK)
            x2 = xdt_b[:, pl_]
            x_bd = jnp.where(bd, jnp.concatenate([x2, x2], axis=0), jnp.zeros((), BF16))
            y_diag.append(_dot(w[:, wl], x_bd))
        y = jnp.concatenate(y_diag, axis=1) + y_off + dexp_ref[lrow, gl] * xs[:, gl]
        zg = z_ref[rows, gl]
        y = y * _silu(zg)
        y = y * lax.rsqrt(jnp.mean(y * y, axis=-1, keepdims=True) + RMS_EPS) * snw_ref[lrow, gl]
        out_ref[rows, GLA_DV + g * SSD_GROUP_W:GLA_DV + (g + 1) * SSD_GROUP_W] = y.astype(BF16)


def _mixer_constants():
    l = np.arange(CHUNK)
    tri = (l[None, :] <= l[:, None]).astype(np.float32)
    lane = np.arange(SSD_D_INNER)
    m_of_lane = lane % SSD_HEADDIM
    trile = (m_of_lane[None, :] <= l[:, None]).astype(np.float32)
    diage = (m_of_lane[None, :] == l[:, None]).astype(np.float32)
    expand = np.zeros((DT_LANE + SSD_HEADS, SSD_D_INNER), np.float32)
    expand[DT_LANE + lane // SSD_HEADDIM, lane] = 1.0
    r = np.arange(2 * SSD_HEADDIM)
    bd = ((r[:, None] // SSD_HEADDIM) == (r[None, :] // SSD_HEADDIM)).astype(np.float32)
    return (jnp.asarray(expand, BF16), jnp.asarray(tri, BF16), jnp.asarray(trile), jnp.asarray(diage),
            jnp.asarray(bd))


def _mixer(qkvog, z, xbc, sm, wg_all, per_layer, ffn_w_up, ffn_w_down, layer, chunks_per_step=4):
    assert CHUNK == SSD_HEADDIM
    s = qkvog.shape[0]
    consts = _mixer_constants()
    tm = chunks_per_step * CHUNK
    steps = s // tm
    row = lambda n: pl.BlockSpec((tm, n), lambda i: (i, 0))
    bg, gnw, cw, cb, dtb, a_log, dexp, snw = per_layer
    layer_slab = lambda t: pl.BlockSpec((None,) + t.shape[1:], lambda i: (layer, 0, 0))
    order = (wg_all, bg, gnw, cw, cb, dtb, a_log, dexp, snw) + consts
    in_specs = [row(QKVOG_W), row(SSD_D_INNER), row(SSD_CONV_DIM), row(SMALL_W),
                layer_slab(wg_all), _whole(bg), _whole(gnw), layer_slab(cw)]
    in_specs += [_whole(t) for t in order[4:]]
    up_rows, up_cols = ffn_w_up.shape[1] // steps, ffn_w_up.shape[2]
    dn_rows, dn_cols = ffn_w_down.shape[1] // steps, ffn_w_down.shape[2]
    assert up_rows * steps == ffn_w_up.shape[1] and up_rows % (2 * SUBLANES) == 0
    assert dn_rows * steps == ffn_w_down.shape[1] and dn_rows % (2 * SUBLANES) == 0
    in_specs += [pl.BlockSpec((None, up_rows, up_cols), lambda i: (layer, i, 0)),
                 pl.BlockSpec((None, dn_rows, dn_cols), lambda i: (layer, i, 0))]
    return pl.pallas_call(
        functools.partial(_mixer_kernel, layer=layer),
        grid=(steps,),
        in_specs=in_specs,
        out_specs=[row(D_MODEL), pl.BlockSpec((up_rows, up_cols), lambda i: (i, 0)),
                   pl.BlockSpec((dn_rows, dn_cols), lambda i: (i, 0))],
        out_shape=[jax.ShapeDtypeStruct((s, D_MODEL), BF16),
                   jax.ShapeDtypeStruct(ffn_w_up.shape[1:], BF16),
                   jax.ShapeDtypeStruct(ffn_w_down.shape[1:], BF16)],
        scratch_shapes=[
            pltpu.VMEM((GLA_HEADS, GLA_HEAD_V, GLA_HEAD_K), F32),
            pltpu.VMEM((SSD_GROUPS, SSD_STATE, SSD_GROUP_W), F32),
            pltpu.VMEM((SUBLANES, SSD_CONV_DIM), F32),
        ],
        compiler_params=_params(("arbitrary",)),
        name="mixer",
    )(qkvog, z, xbc, sm, *order, ffn_w_up, ffn_w_down)


def _proj_ln_kernel(a_ref, w_ref, x_ref, g_ref, b_ref, o_ref, wb_ref, *, n_load, layer):
    s = pl.program_id(0)
    lrow = slice(layer, layer + 1)

    @pl.when(s < n_load)
    def _():
        wb_ref[_chunk_rows(s), :] = w_ref[...].astype(BF16)

    @pl.when(s >= n_load)
    def _():
        half = a_ref.shape[0] // ROW_GROUPS
        for r in range(ROW_GROUPS):
            rows = slice(r * half, (r + 1) * half)
            h = _dot(a_ref[rows, :], wb_ref[...])
            o_ref[rows, :] = _layer_norm(ALPHA * x_ref[rows, :] + h, g_ref[lrow, :], b_ref[lrow, :])


def _proj_ln(a, w_all, x, g_all, b_all, layer, tm=512):
    s, d = x.shape
    kdim = a.shape[1]
    n_load = kdim // LOAD_ROWS
    return pl.pallas_call(
        functools.partial(_proj_ln_kernel, n_load=n_load, layer=layer),
        grid=(n_load + s // tm,),
        in_specs=[_seq_tile(n_load, tm, kdim), _weight_chunk(layer, n_load, d), _seq_tile(n_load, tm, d),
                  _whole(g_all), _whole(b_all)],
        out_specs=_seq_tile(n_load, tm, d),
        out_shape=jax.ShapeDtypeStruct((s, d), F32),
        scratch_shapes=[pltpu.VMEM((kdim, d), BF16)],
        compiler_params=_params(("arbitrary",)),
        name="proj_ln",
    )(a, w_all, x, g_all, b_all)


def _xattn_kernel(x_ref, mem_ref, wq_ref, wk_ref, wv_ref, wo_ref, g_ref, b_ref, o_ref,
                  wqb_ref, wob_ref, kacc_ref, vacc_ref, kb_ref, vb_ref, att_ref, *, n_load, layer):
    s = pl.program_id(0)
    lrow = slice(layer, layer + 1)

    @pl.when(s == 0)
    def _():
        kacc_ref[...] = jnp.zeros_like(kacc_ref)
        vacc_ref[...] = jnp.zeros_like(vacc_ref)

    @pl.when(s < n_load)
    def _():
        rows = _chunk_rows(s)
        wqb_ref[rows, :] = wq_ref[...].astype(BF16)
        wob_ref[rows, :] = wo_ref[...].astype(BF16)
        mb = mem_ref[...].astype(BF16)
        kacc_ref[...] += _dot(mb, wk_ref[...].astype(BF16))
        vacc_ref[...] += _dot(mb, wv_ref[...].astype(BF16))

    @pl.when(s == n_load - 1)
    def _():
        kb_ref[...] = kacc_ref[...].astype(BF16)
        vb_ref[...] = vacc_ref[...].astype(BF16)

    @pl.when(s >= n_load)
    def _():
        q = _dot(x_ref[...].astype(BF16), wqb_ref[...])
        for h in range(XA_HEADS):
            hs = slice(h * XA_HEAD_DIM, (h + 1) * XA_HEAD_DIM)
            sc = _dotg(q[:, hs].astype(BF16), kb_ref[:, hs], NT) * (XA_HEAD_DIM ** -0.5)
            e = jnp.exp(sc - jnp.max(sc, axis=-1, keepdims=True))
            p = e / jnp.sum(e, axis=-1, keepdims=True)
            att_ref[:, hs] = _dot(p.astype(BF16), vb_ref[:, hs]).astype(BF16)
        group = x_ref.shape[0] // ROW_GROUPS
        for r in range(ROW_GROUPS):
            rows = slice(r * group, (r + 1) * group)
            h_out = _dot(att_ref[rows, :], wob_ref[...])
            o_ref[rows, :] = _layer_norm(ALPHA * x_ref[rows, :] + h_out, g_ref[lrow, :], b_ref[lrow, :])


def _xattn(x, mem, wq_all, wk_all, wv_all, wo_all, g_all, b_all, layer, tm=512):
    s, d = x.shape
    m = mem.shape[1]
    n_load = d // LOAD_ROWS
    mem_spec = pl.BlockSpec((None, m, LOAD_ROWS), lambda t: (0, 0, jnp.minimum(t, n_load - 1)))
    wchunk = _weight_chunk(layer, n_load, d)
    return pl.pallas_call(
        functools.partial(_xattn_kernel, n_load=n_load, layer=layer),
        grid=(n_load + s // tm,),
        in_specs=[_seq_tile(n_load, tm, d), mem_spec, wchunk, wchunk, wchunk, wchunk,
                  _whole(g_all), _whole(b_all)],
        out_specs=_seq_tile(n_load, tm, d),
        out_shape=jax.ShapeDtypeStruct((s, d), F32),
        scratch_shapes=[pltpu.VMEM((d, d), BF16), pltpu.VMEM((d, d), BF16),
                        pltpu.VMEM((m, d), F32), pltpu.VMEM((m, d), F32),
                        pltpu.VMEM((m, d), BF16), pltpu.VMEM((m, d), BF16),
                        pltpu.VMEM((tm, d), BF16)],
        compiler_params=_params(("arbitrary",)),
        name="xattn",
    )(x, mem, wq_all, wk_all, wv_all, wo_all, g_all, b_all)


def _ffn_kernel(x_ref, wg_ref, wv_ref, wd_ref, cw_ref, cb_ref, g_ref, b_ref, o_ref,
                xb_ref, carry_ref, gext_ref, *, layer):
    i = pl.program_id(0)
    j = pl.program_id(1)
    tm = x_ref.shape[0]
    lrow = slice(layer, layer + 1)

    @pl.when(j == 0)
    def _():
        xb_ref[...] = x_ref[...].astype(BF16)
        o_ref[...] = jnp.zeros_like(o_ref)

    @pl.when(i == 0)
    def _():
        carry_ref[j] = jnp.zeros(carry_ref.shape[1:], F32)

    def hidden():
        xb = xb_ref[...]
        gate = _dot(xb, wg_ref[...])
        val = _dot(xb, wv_ref[...])
        gext_ref[0:SUBLANES, :] = carry_ref[j]
        gext_ref[SUBLANES:SUBLANES + tm, :] = gate
        carry_ref[j] = gate[tm - SUBLANES:tm, :]
        conv = cb_ref[lrow, :] + cw_ref[FFN_CONV - 1:FFN_CONV, :] * gate
        for t in range(FFN_CONV - 1):
            off = SUBLANES - (FFN_CONV - 1) + t
            conv = conv + cw_ref[t:t + 1, :] * gext_ref[off:off + tm, :]
        act = 0.5 * conv * (1.0 + lax.erf(conv * math.sqrt(0.5)))
        return (act * val).astype(BF16)

    last = pl.num_programs(1) - 1

    @pl.when(j < last)
    def _():
        o_ref[...] += _dot(hidden(), wd_ref[...])

    @pl.when(j == last)
    def _():
        hid = hidden()
        group = tm // ROW_GROUPS
        for r in range(ROW_GROUPS):
            rows = slice(r * group, (r + 1) * group)
            total = o_ref[rows, :] + _dot(hid[rows, :], wd_ref[...])
            o_ref[rows, :] = _layer_norm(ALPHA * x_ref[rows, :] + total, g_ref[lrow, :], b_ref[lrow, :])


def _ffn(x, wup, wdn, cw_all, cb_all, g_all, b_all, layer, tm=512, tf=512):
    s, d = x.shape
    f = wdn.shape[0]
    nf = f // tf
    nl = cb_all.shape[0]
    return pl.pallas_call(
        functools.partial(_ffn_kernel, layer=layer),
        grid=(s // tm, nf),
        in_specs=[pl.BlockSpec((tm, d), lambda i, j: (i, 0)),
                  pl.BlockSpec((d, tf), lambda i, j: (0, j)),
                  pl.BlockSpec((d, tf), lambda i, j: (0, j + nf)),
                  pl.BlockSpec((tf, d), lambda i, j: (j, 0)),
                  pl.BlockSpec((None, FFN_CONV, tf), lambda i, j: (layer, 0, j)),
                  pl.BlockSpec((nl, tf), lambda i, j: (0, j)),
                  pl.BlockSpec((nl, d), lambda i, j: (0, 0)),
                  pl.BlockSpec((nl, d), lambda i, j: (0, 0))],
        out_specs=pl.BlockSpec((tm, d), lambda i, j: (i, 0)),
        out_shape=jax.ShapeDtypeStruct((s, d), F32),
        scratch_shapes=[pltpu.VMEM((tm, d), BF16),
                        pltpu.VMEM((nf, SUBLANES, tf), F32), pltpu.VMEM((SUBLANES + tm, tf), F32)],
        compiler_params=_params(("arbitrary", "arbitrary")),
        name="ffn",
    )(x, wup, wup, wdn, cw_all, cb_all, g_all, b_all)


def _at_lanes(table, start):
    return jnp.pad(table, ((0, 0), (start, LANES - start - table.shape[1])))


def kernel(x, mem, w_in, gla_w_gate, gla_b_gate, gla_norm_w, ssd_conv_w, ssd_conv_b, ssd_dt_bias, ssd_a_log, ssd_d, ssd_norm_w, w_out, ln_mix_g, ln_mix_b, xa_wq, xa_wk, xa_wv, xa_wo, ln_xa_g, ln_xa_b, ffn_w_up, ffn_conv_w, ffn_conv_b, ffn_w_down, ln_ffn_g, ln_ffn_b):
    assert x.shape == (1, SEQ, D_MODEL) and mem.shape == (1, N_MEM, D_MODEL)
    assert w_in.shape == (DEPTH, D_MODEL, D_IN)
    xs = x[0]
    wg_all = jnp.pad(gla_w_gate, ((0, 0), (0, LANES - GLA_RANK), (0, 0))).astype(BF16)
    mixer_tables = (gla_b_gate, gla_norm_w, ssd_conv_w, ssd_conv_b, _at_lanes(ssd_dt_bias, DT_LANE),
                    _at_lanes(ssd_a_log, DT_LANE), jnp.repeat(ssd_d, SSD_HEADDIM, axis=1), ssd_norm_w)
    w_in_t = jnp.swapaxes(w_in, 1, 2)
    for l in range(DEPTH):
        qkvog, z, xbc, sm = _inproj(xs, w_in_t, l)
        mixed, wup, wdn = _mixer(qkvog, z, xbc, sm, wg_all, mixer_tables, ffn_w_up, ffn_w_down, l)
        xs = _proj_ln(mixed, w_out, xs, ln_mix_g, ln_mix_b, l)
        xs = _xattn(xs, mem, xa_wq, xa_wk, xa_wv, xa_wo, ln_xa_g, ln_xa_b, l)
        xs = _ffn(xs, wup, wdn, ffn_conv_w, ffn_conv_b, ln_ffn_g, ln_ffn_b, l)
    return xs[None]
```
